```python
import jax, jax.numpy as jnp
from jax import lax
import numpy as np

D_MODEL = 1024
BATCH = 2
SEQ = 8192
DEPTH = 2

HEAD_DIM = 64
RET_H = (3 * D_MODEL // 8) // HEAD_DIM
RET_W = RET_H * HEAD_DIM
RET_CHUNK = 128
ROPE_BASE = 10000.0
RWKV_H = (3 * D_MODEL // 8) // HEAD_DIM
RWKV_W = RWKV_H * HEAD_DIM
RWKV_W_LORA = 64
RWKV_A_LORA = 64
RWKV_G_LORA = 128
RWKV_GN_EPS = 64e-5
LRU_H = (D_MODEL // 4) // HEAD_DIM
LRU_W = LRU_H * HEAD_DIM
LRU_CONV = 4
LRU_C = 8.0
MIX_W = RET_W + RWKV_W + LRU_W
RET_SLAB = 4 * RET_W
RWKV_SLAB = 3 * RWKV_W + RWKV_W_LORA + RWKV_A_LORA + RWKV_G_LORA
LRU_SLAB = 2 * LRU_W
IN_W = RET_SLAB + RWKV_SLAB + LRU_SLAB
N_GROUPS = 4
EXPERTS_PER_GROUP = 8
N_EXPERTS = N_GROUPS * EXPERTS_PER_GROUP
TOPK_IN_GROUP = 2
D_EXPERT = D_MODEL // 2
MOE_BLOCK = 256
LN_EPS = 1e-5

kernel_name = "hybrid_ret_rwkv7_rglru_hmoe_deepnorm"


def layer_norm(x, g, b):
    xf = x.astype(jnp.float32)
    mu = jnp.mean(xf, -1, keepdims=True)
    var = jnp.mean(jnp.square(xf - mu), -1, keepdims=True)
    return ((xf - mu) * lax.rsqrt(var + LN_EPS) * g + b).astype(x.dtype)


def head_norm(y, eps):
    mu = jnp.mean(y, -1, keepdims=True)
    var = jnp.mean(jnp.square(y - mu), -1, keepdims=True)
    return (y - mu) * lax.rsqrt(var + eps)


def rotary(t):
    S, Dh = t.shape[1], t.shape[-1]
    inv = 1.0 / (ROPE_BASE ** (jnp.arange(0, Dh, 2, dtype=jnp.float32) / Dh))
    ang = jnp.arange(S, dtype=jnp.float32)[:, None] * inv[None, :]
    cos = jnp.cos(ang)[None, :, None, :]
    sin = jnp.sin(ang)[None, :, None, :]
    t1, t2 = t[..., : Dh // 2], t[..., Dh // 2 :]
    return jnp.concatenate([t1 * cos - t2 * sin, t1 * sin + t2 * cos], axis=-1)


def retention_chunkwise(q, k, v):
    B, S, H, Dh = q.shape
    C = RET_CHUNK
    nc = S // C
    log_g = jnp.log1p(-jnp.exp2(-5.0 - jnp.arange(H, dtype=jnp.float32)))
    pos = jnp.arange(C, dtype=jnp.float32)
    diff = pos[:, None] - pos[None, :]
    causal = diff >= 0
    d_intra = jnp.where(causal[None], jnp.exp(jnp.where(causal, diff, 0.0)[None] * log_g[:, None, None]), 0.0)
    qc = q.reshape(B, nc, C, H, Dh)
    kc = k.reshape(B, nc, C, H, Dh)
    vc = v.reshape(B, nc, C, H, Dh)
    scores = jnp.einsum('bnchd,bnmhd->bnhcm', qc, kc) * d_intra[None, None]
    intra = jnp.einsum('bnhcm,bnmhd->bnchd', scores, vc)
    k_w = jnp.exp((C - 1.0 - pos)[:, None] * log_g[None, :])
    chunk_kv = jnp.einsum('bnchd,bnche->nbhde', kc * k_w[:, :, None], vc)
    g_chunk = jnp.exp(C * log_g)[None, :, None, None]

    def step(state, kv):
        return state * g_chunk + kv, state

    _, prev = lax.scan(step, jnp.zeros((B, H, Dh, Dh), jnp.float32), chunk_kv)
    q_w = jnp.exp((pos + 1.0)[:, None] * log_g[None, :])
    cross = jnp.einsum('bnchd,nbhde->bnche', qc * q_w[:, :, None], prev)
    return (intra + cross).reshape(B, S, H, Dh)


def rwkv7_recurrence(r, w, k, v, kk, a):
    B, S, H, N = r.shape

    def step(state, inp):
        r_t, w_t, k_t, v_t, kk_t, a_t = inp
        sa = jnp.einsum('bhvk,bhk->bhv', state, -kk_t)
        state = (state * w_t[:, :, None, :] + sa[..., None] * (kk_t * a_t)[:, :, None, :]
                 + v_t[..., None] * k_t[:, :, None, :])
        return state, jnp.einsum('bhvk,bhk->bhv', state, r_t)

    xs = tuple(jnp.swapaxes(t, 0, 1) for t in (r, w, k, v, kk, a))
    _, y = lax.scan(step, jnp.zeros((B, H, N, N), jnp.float32), xs)
    return jnp.swapaxes(y, 0, 1)


def token_shift_mix(p, mu):
    prev = jnp.pad(p, ((0, 0), (1, 0), (0, 0)))[:, :-1]
    return p + (prev - p) * mu


def hybrid_mixer(h, w_in, w_out, rwkv_mu, rwkv_w0, rwkv_w2, rwkv_a0, rwkv_a2, rwkv_g2, rwkv_k_k, rwkv_k_a,
                 rwkv_r_k, rwkv_gn_g, rwkv_gn_b, lru_conv_w, lru_conv_b, lru_wa, lru_ba, lru_wx, lru_bx, lru_lambda):
    B, S, _ = h.shape
    f32 = jnp.float32
    proj = h @ w_in
    p_ret, p_rwkv, p_lru = jnp.split(proj, [RET_SLAB, RET_SLAB + RWKV_SLAB], axis=-1)

    q, k, v, g_ret = jnp.split(p_ret.astype(f32), 4, axis=-1)
    q = rotary(q.reshape(B, S, RET_H, HEAD_DIM))
    k = rotary(k.reshape(B, S, RET_H, HEAD_DIM)) * (HEAD_DIM ** -0.5)
    v = v.reshape(B, S, RET_H, HEAD_DIM)
    y_ret = head_norm(retention_chunkwise(q, k, v), LN_EPS).reshape(B, S, RET_W) * jax.nn.silu(g_ret)

    z = token_shift_mix(p_rwkv.astype(f32), rwkv_mu)
    cuts = [RWKV_W, 2 * RWKV_W, 3 * RWKV_W, 3 * RWKV_W + RWKV_W_LORA, 3 * RWKV_W + RWKV_W_LORA + RWKV_A_LORA]
    r, kr, vr, wl, al, gl = jnp.split(z, cuts, axis=-1)
    w_log = -jax.nn.softplus(-(rwkv_w0 + jnp.tanh(wl) @ rwkv_w2)) - 0.5
    decay = jnp.exp(-jnp.exp(w_log))
    a = jax.nn.sigmoid(rwkv_a0 + al @ rwkv_a2)
    g_rw = jax.nn.sigmoid(gl) @ rwkv_g2
    hs = lambda t: t.reshape(B, S, RWKV_H, HEAD_DIM)
    kk = hs(kr * rwkv_k_k)
    kk = kk / jnp.maximum(jnp.sqrt(jnp.sum(kk * kk, -1, keepdims=True)), 1e-12)
    kr = kr * (1.0 + (a - 1.0) * rwkv_k_a)
    r_h, k_h, v_h = hs(r), hs(kr), hs(vr)
    y = rwkv7_recurrence(r_h, hs(decay), k_h, v_h, kk, hs(a))
    y = head_norm(y, RWKV_GN_EPS).reshape(B, S, RWKV_W) * rwkv_gn_g + rwkv_gn_b
    bonus = (jnp.sum(r_h * k_h * rwkv_r_k, -1, keepdims=True) * v_h).reshape(B, S, RWKV_W)
    y_rwkv = (y + bonus) * g_rw

    xl, gate = jnp.split(p_lru.astype(f32), 2, axis=-1)
    xl = lax.conv_general_dilated(xl, lru_conv_w.astype(f32)[:, None, :], (1,), [(LRU_CONV - 1, 0)],
                                  dimension_numbers=('NWC', 'WIO', 'NWC'), feature_group_count=LRU_W) + lru_conv_b
    xb = xl.reshape(B, S, LRU_H, HEAD_DIM)
    r_gate = jax.nn.sigmoid(jnp.einsum('bshi,hij->bshj', xb, lru_wa).reshape(B, S, LRU_W) + lru_ba)
    i_gate = jax.nn.sigmoid(jnp.einsum('bshi,hij->bshj', xb, lru_wx).reshape(B, S, LRU_W) + lru_bx)
    log_a = -LRU_C * r_gate * jax.nn.softplus(-lru_lambda)
    a_t = jnp.exp(log_a)
    b_t = jnp.sqrt(-jnp.expm1(2.0 * log_a)) * (i_gate * xl)
    _, h_lru = lax.associative_scan(lambda c1, c2: (c1[0] * c2[0], c2[0] * c1[1] + c2[1]), (a_t, b_t), axis=1)
    y_lru = h_lru * jax.nn.gelu(gate)

    y_cat = jnp.concatenate([y_ret, y_rwkv, y_lru], axis=-1).astype(h.dtype)
    return y_cat @ w_out


def hier_moe(x, moe_wg, moe_bg, moe_we, moe_be, moe_w_gate, moe_w_up, moe_w_down):
    B, S, D = x.shape
    T = B * S
    f32 = jnp.float32
    xf = x.reshape(T, D)
    gl = (xf @ moe_wg).astype(f32) + moe_bg
    pg = jax.nn.softmax(gl, axis=-1)
    g_sel = jnp.argmax(gl, axis=-1)
    g_gate = jnp.take_along_axis(pg, g_sel[:, None], axis=1)[:, 0]
    el = ((xf @ moe_we).astype(f32) + moe_be).reshape(T, N_GROUPS, EXPERTS_PER_GROUP)
    el = jnp.take_along_axis(el, g_sel[:, None, None], axis=1)[:, 0]
    top_p, top_i = lax.top_k(jax.nn.softmax(el, axis=-1), TOPK_IN_GROUP)
    top_p = top_p / jnp.sum(top_p, -1, keepdims=True)
    weights = g_gate[:, None] * top_p
    experts = g_sel[:, None] * EXPERTS_PER_GROUP + top_i

    A = T * TOPK_IN_GROUP
    e_flat = experts.reshape(A)
    t_flat = jnp.repeat(jnp.arange(T, dtype=jnp.int32), TOPK_IN_GROUP)
    w_flat = weights.reshape(A)
    order = jnp.argsort(e_flat)
    e_s, t_s, w_s = e_flat[order], t_flat[order], w_flat[order]
    counts = jnp.bincount(e_flat, length=N_EXPERTS)
    starts = jnp.cumsum(counts) - counts
    padded = ((counts + MOE_BLOCK - 1) // MOE_BLOCK) * MOE_BLOCK
    pends = jnp.cumsum(padded)
    pstarts = pends - padded
    dest = pstarts[e_s] + (jnp.arange(A) - starts[e_s])
    n_blocks = -(-A // MOE_BLOCK) + N_EXPERTS
    P = n_blocks * MOE_BLOCK
    slot_tok = jnp.full((P,), T, jnp.int32).at[dest].set(t_s.astype(jnp.int32))
    slot_w = jnp.zeros((P,), x.dtype).at[dest].set(w_s.astype(x.dtype))
    block_exp = jnp.minimum(jnp.searchsorted(pends, jnp.arange(n_blocks) * MOE_BLOCK, side='right'), N_EXPERTS - 1)
    x_pad = jnp.concatenate([xf, jnp.zeros((1, D), x.dtype)], axis=0)
    xb = x_pad[slot_tok].reshape(n_blocks, MOE_BLOCK, D)

    def expert_block(args):
        xblk, e = args
        hid = jax.nn.silu(xblk @ moe_w_gate[e]) * (xblk @ moe_w_up[e])
        return hid @ moe_w_down[e]

    yb = lax.map(expert_block, (xb, block_exp)).reshape(P, D)
    y = jnp.zeros((T + 1, D), x.dtype).at[slot_tok].add(yb * slot_w[:, None])[:T]
    return y.reshape(B, S, D)


def setup_inputs(seed: int = 0) -> dict:
    key = jax.random.key(seed)
    ks = iter(jax.random.split(key, 40))
    L, D = DEPTH, D_MODEL
    f32 = jnp.float32
    beta = (8.0 * DEPTH) ** -0.25

    def nrm(shape, scale):
        return scale * jax.random.normal(next(ks), shape, f32)

    def uni(shape, lo, hi):
        return jax.random.uniform(next(ks), shape, f32, lo, hi)

    x = nrm((BATCH, SEQ, D), 1.0)
    ln_in_g = 1.0 + nrm((D,), 0.02)
    ln_in_b = nrm((D,), 0.02)
    w_in = nrm((L, D, IN_W), D ** -0.5)
    w_out = nrm((L, MIX_W, D), beta * MIX_W ** -0.5)
    rwkv_mu = uni((L, RWKV_SLAB), 0.0, 1.0)
    rwkv_w0 = uni((L, RWKV_W), -6.0, 1.0)
    rwkv_w2 = nrm((L, RWKV_W_LORA, RWKV_W), 0.1 * RWKV_W_LORA ** -0.5)
    rwkv_a0 = nrm((L, RWKV_W), 0.1)
    rwkv_a2 = nrm((L, RWKV_A_LORA, RWKV_W), 0.1 * RWKV_A_LORA ** -0.5)
    rwkv_g2 = nrm((L, RWKV_G_LORA, RWKV_W), RWKV_G_LORA ** -0.5)
    rwkv_k_k = 0.85 + nrm((L, RWKV_W), 0.05)
    rwkv_k_a = 1.0 + nrm((L, RWKV_W), 0.05)
    rwkv_r_k = nrm((L, RWKV_H, HEAD_DIM), 0.1)
    rwkv_gn_g = 1.0 + nrm((L, RWKV_W), 0.02)
    rwkv_gn_b = nrm((L, RWKV_W), 0.02)
    lru_conv_w = nrm((L, LRU_CONV, LRU_W), LRU_CONV ** -0.5)
    lru_conv_b = nrm((L, LRU_W), 0.02)
    lru_wa = nrm((L, LRU_H, HEAD_DIM, HEAD_DIM), HEAD_DIM ** -0.5)
    lru_ba = nrm((L, LRU_W), 0.02)
    lru_wx = nrm((L, LRU_H, HEAD_DIM, HEAD_DIM), HEAD_DIM ** -0.5)
    lru_bx = nrm((L, LRU_W), 0.02)
    a_init = uni((L, LRU_W), 0.9, 0.999)
    s = a_init ** (1.0 / LRU_C)
    lru_lambda = jnp.log(s) - jnp.log1p(-s)
    ln1_g = 1.0 + nrm((L, D), 0.02)
    ln1_b = nrm((L, D), 0.02)
    moe_wg = nrm((L, D, N_GROUPS), D ** -0.5)
    moe_bg = nrm((L, N_GROUPS), 0.01)
    moe_we = nrm((L, D, N_EXPERTS), D ** -0.5)
    moe_be = nrm((L, N_EXPERTS), 0.01)
    moe_w_gate = nrm((L, N_EXPERTS, D, D_EXPERT), D ** -0.5)
    moe_w_up = nrm((L, N_EXPERTS, D, D_EXPERT), D ** -0.5)
    moe_w_down = nrm((L, N_EXPERTS, D_EXPERT, D), beta * D_EXPERT ** -0.5)
    ln2_g = 1.0 + nrm((L, D), 0.02)
    ln2_b = nrm((L, D), 0.02)
    return {"x": x, "ln_in_g": ln_in_g, "ln_in_b": ln_in_b, "w_in": w_in, "w_out": w_out,
            "rwkv_mu": rwkv_mu, "rwkv_w0": rwkv_w0, "rwkv_w2": rwkv_w2, "rwkv_a0": rwkv_a0,
            "rwkv_a2": rwkv_a2, "rwkv_g2": rwkv_g2, "rwkv_k_k": rwkv_k_k, "rwkv_k_a": rwkv_k_a,
            "rwkv_r_k": rwkv_r_k, "rwkv_gn_g": rwkv_gn_g, "rwkv_gn_b": rwkv_gn_b,
            "lru_conv_w": lru_conv_w, "lru_conv_b": lru_conv_b, "lru_wa": lru_wa, "lru_ba": lru_ba,
            "lru_wx": lru_wx, "lru_bx": lru_bx, "lru_lambda": lru_lambda, "ln1_g": ln1_g, "ln1_b": ln1_b,
            "moe_wg": moe_wg, "moe_bg": moe_bg, "moe_we": moe_we, "moe_be": moe_be,
            "moe_w_gate": moe_w_gate, "moe_w_up": moe_w_up, "moe_w_down": moe_w_down,
            "ln2_g": ln2_g, "ln2_b": ln2_b}


def reference(x, ln_in_g, ln_in_b, w_in, w_out, rwkv_mu, rwkv_w0, rwkv_w2, rwkv_a0, rwkv_a2, rwkv_g2,
              rwkv_k_k, rwkv_k_a, rwkv_r_k, rwkv_gn_g, rwkv_gn_b, lru_conv_w, lru_conv_b, lru_wa, lru_ba,
              lru_wx, lru_bx, lru_lambda, ln1_g, ln1_b, moe_wg, moe_bg, moe_we, moe_be, moe_w_gate,
              moe_w_up, moe_w_down, ln2_g, ln2_b):
    alpha = (2.0 * DEPTH) ** 0.25
    h = layer_norm(x, ln_in_g, ln_in_b)
    for l in range(DEPTH):
        mix = hybrid_mixer(h, w_in[l], w_out[l], rwkv_mu[l], rwkv_w0[l], rwkv_w2[l], rwkv_a0[l], rwkv_a2[l],
                           rwkv_g2[l], rwkv_k_k[l], rwkv_k_a[l], rwkv_r_k[l], rwkv_gn_g[l], rwkv_gn_b[l],
                           lru_conv_w[l], lru_conv_b[l], lru_wa[l], lru_ba[l], lru_wx[l], lru_bx[l], lru_lambda[l])
        h = layer_norm(alpha * h + mix.astype(h.dtype), ln1_g[l], ln1_b[l])
        ffn = hier_moe(h, moe_wg[l], moe_bg[l], moe_we[l], moe_be[l], moe_w_gate[l], moe_w_up[l], moe_w_down[l])
        h = layer_norm(alpha * h + ffn.astype(h.dtype), ln2_g[l], ln2_b[l])
    return h
```

```python
import functools

import jax
import jax.numpy as jnp
from jax import lax
from jax.experimental import pallas as pl
from jax.experimental.pallas import tpu as pltpu

F32 = jnp.float32
BF16 = jnp.bfloat16
I32 = jnp.int32
HIGHEST = lax.Precision.HIGHEST

D_MODEL = 1024
DEPTH = 2
HEAD_DIM = 64
HEAD_SHIFT = 6
RET_H = 6
RET_W = RET_H * HEAD_DIM
RET_CHUNK = 128
ROPE_BASE = 10000.0
RWKV_H = 6
RWKV_W = RWKV_H * HEAD_DIM
RWKV_W_LORA = 64
RWKV_A_LORA = 64
RWKV_G_LORA = 128
RWKV_GN_EPS = 64e-5
LRU_H = 4
LRU_W = LRU_H * HEAD_DIM
LRU_CONV = 4
LRU_C = 8.0
RET_SLAB = 4 * RET_W
RWKV_SLAB = 3 * RWKV_W + RWKV_W_LORA + RWKV_A_LORA + RWKV_G_LORA
LRU_SLAB = 2 * LRU_W
IN_W = RET_SLAB + RWKV_SLAB + LRU_SLAB
N_GROUPS = 4
EXPERTS_PER_GROUP = 8
GROUP_SHIFT = 3
N_EXPERTS = N_GROUPS * EXPERTS_PER_GROUP
D_EXPERT = D_MODEL // 2
MOE_BLOCK = 256
LN_EPS = 1e-5
ALPHA = (2.0 * DEPTH) ** 0.25

LANES = 128
SUBLANES = 8
N_PAIRS = RET_W // LANES
RWKV_CHUNK = 64
VMEM_LIMIT = 56 * 1024 * 1024


def _cparams(n_axes, vmem=VMEM_LIMIT):
    return pltpu.CompilerParams(dimension_semantics=("arbitrary",) * n_axes, vmem_limit_bytes=vmem)


def _dot_bf(a, b):
    return jnp.dot(a.astype(BF16), b.astype(BF16), preferred_element_type=F32)


def _dot_hi(a, b):
    return jnp.dot(a, b, precision=HIGHEST, preferred_element_type=F32)


def _dot_nt_hi(a, b):
    return lax.dot_general(a, b, (((1,), (1,)), ((), ())), precision=HIGHEST, preferred_element_type=F32)


def _dot_tn_hi(a, b):
    return jnp.dot(a.T, b, precision=HIGHEST, preferred_element_type=F32)


def _sigmoid(x):
    return 1.0 / (1.0 + jnp.exp(-x))


def _silu(x):
    return x * _sigmoid(x)


def _softplus(x):
    return jnp.maximum(x, 0.0) + jnp.log1p(jnp.exp(-jnp.abs(x)))


def _gelu_tanh(x):
    return 0.5 * x * (1.0 + jnp.tanh(0.7978845608028654 * (x + 0.044715 * (x * x * x))))


def _layer_norm(x, g, b):
    mu = jnp.mean(x, -1, keepdims=True)
    xc = x - mu
    var = jnp.mean(xc * xc, -1, keepdims=True)
    return xc * lax.rsqrt(var + LN_EPS) * g + b


def _pair_masks():
    lane = lax.broadcasted_iota(I32, (1, LANES), 1)
    m_a = lane < HEAD_DIM
    row = lax.broadcasted_iota(I32, (LANES, LANES), 0)
    col = lax.broadcasted_iota(I32, (LANES, LANES), 1)
    same_head = (row >> HEAD_SHIFT) == (col >> HEAD_SHIFT)
    return m_a, row, col, same_head


def _head_norm_pair(y, same_head, eps):
    avg = jnp.where(same_head, 1.0 / HEAD_DIM, 0.0).astype(F32)
    mu = _dot_hi(y, avg)
    yc = y - mu
    var = _dot_hi(yc * yc, avg)
    return yc * lax.rsqrt(var + eps)


def _inproj_ln_kernel(x_ref, g_ref, b_ref, w_ref, h_ref, pr_ref, pw_ref, pl_ref):
    h = _layer_norm(x_ref[...], g_ref[...], b_ref[...])
    h_ref[...] = h
    p = jnp.dot(h.astype(BF16), w_ref[...], preferred_element_type=F32)
    pr_ref[...] = p[:, :RET_SLAB]
    pw_ref[...] = p[:, RET_SLAB:RET_SLAB + RWKV_SLAB]
    pl_ref[...] = p[:, RET_SLAB + RWKV_SLAB:]


def _inproj_kernel(h_ref, w_ref, pr_ref, pw_ref, pl_ref):
    p = jnp.dot(h_ref[...].astype(BF16), w_ref[...], preferred_element_type=F32)
    pr_ref[...] = p[:, :RET_SLAB]
    pw_ref[...] = p[:, RET_SLAB:RET_SLAB + RWKV_SLAB]
    pl_ref[...] = p[:, RET_SLAB + RWKV_SLAB:]


def _in_projection(x2d, w_bf, ln=None):
    t = x2d.shape[0]
    tm = min(512, t)
    row = lambda w: pl.BlockSpec((tm, w), lambda i: (i, 0))
    full = lambda a: pl.BlockSpec(a.shape, lambda i: (0,) * a.ndim)
    outs = [jax.ShapeDtypeStruct((t, RET_SLAB), F32), jax.ShapeDtypeStruct((t, RWKV_SLAB), F32),
            jax.ShapeDtypeStruct((t, LRU_SLAB), F32)]
    out_specs = [row(RET_SLAB), row(RWKV_SLAB), row(LRU_SLAB)]
    if ln is None:
        return pl.pallas_call(
            _inproj_kernel, grid=(t // tm,), in_specs=[row(D_MODEL), full(w_bf)], out_specs=out_specs,
            out_shape=outs, compiler_params=_cparams(1), name="in_proj")(x2d, w_bf)
    g, b = ln
    return pl.pallas_call(
        _inproj_ln_kernel, grid=(t // tm,), in_specs=[row(D_MODEL), full(g), full(b), full(w_bf)],
        out_specs=[row(D_MODEL)] + out_specs, out_shape=[jax.ShapeDtypeStruct((t, D_MODEL), F32)] + outs,
        compiler_params=_cparams(1), name="ln_in_proj")(x2d, g, b, w_bf)


def _swap_halves(t):
    n = t.shape[-1]
    lane = lax.broadcasted_iota(I32, (1, n), 1)
    lower = (lane & (HEAD_DIM - 1)) < (HEAD_DIM // 2)
    return jnp.where(lower, pltpu.roll(t, n - HEAD_DIM // 2, 1), pltpu.roll(t, HEAD_DIM // 2, 1))


def _retention_kernel(p_ref, cos_ref, sin_ref, dmat_ref, qw_ref, kw_ref, gch_ref, o_ref, st_ref):
    @pl.when(pl.program_id(1) == 0)
    def _():
        st_ref[...] = jnp.zeros(st_ref.shape, F32)

    tb = p_ref.shape[0]
    m_a, _, _, same_head = _pair_masks()
    cos = jnp.concatenate([cos_ref[...]] * N_PAIRS, axis=-1)
    sin = jnp.concatenate([sin_ref[...]] * N_PAIRS, axis=-1)
    q = p_ref[:, 0:RET_W]
    k = p_ref[:, RET_W:2 * RET_W]
    q = q * cos + _swap_halves(q) * sin
    k = (k * cos + _swap_halves(k) * sin) * (HEAD_DIM ** -0.5)
    c = RET_CHUNK
    for ci in range(tb // c):
        rows = slice(ci * c, (ci + 1) * c)
        for p in range(N_PAIRS):
            lanes = slice(p * LANES, (p + 1) * LANES)
            qp, kp = q[rows, lanes], k[rows, lanes]
            vp = p_ref[rows, 2 * RET_W + p * LANES:2 * RET_W + (p + 1) * LANES]
            gp = p_ref[rows, 3 * RET_W + p * LANES:3 * RET_W + (p + 1) * LANES]
            kb = kp.astype(BF16)
            intra = jnp.zeros((c, LANES), F32)
            for h in range(2):
                m = m_a if h == 0 else jnp.logical_not(m_a)
                s = lax.dot_general(jnp.where(m, qp, 0.0).astype(BF16), kb, (((1,), (1,)), ((), ())),
                                    preferred_element_type=F32)
                s = s * dmat_ref[2 * p + h]
                intra = intra + _dot_bf(s, jnp.where(m, vp, 0.0))
            st = st_ref[p]
            cross = _dot_bf(qp * qw_ref[:, lanes], st)
            kv = jnp.dot((kp * kw_ref[:, lanes]).astype(BF16).T, vp.astype(BF16), preferred_element_type=F32)
            st_ref[p] = st * gch_ref[:, lanes] + jnp.where(same_head, kv, 0.0)
            y = _head_norm_pair(intra + cross, same_head, LN_EPS)
            o_ref[rows, lanes] = y * _silu(gp)


def _retention(p_ret, consts, b, s):
    tb = min(512, s)
    cos_t, sin_t, dmat, qw, kw, gch = consts
    full = lambda a: pl.BlockSpec(a.shape, lambda i, j: (0,) * a.ndim)
    return pl.pallas_call(
        _retention_kernel, grid=(b, s // tb),
        in_specs=[pl.BlockSpec((None, tb, RET_SLAB), lambda i, j: (i, j, 0)),
                  pl.BlockSpec((tb, LANES), lambda i, j: (j, 0)), pl.BlockSpec((tb, LANES), lambda i, j: (j, 0)),
                  full(dmat), full(qw), full(kw), full(gch)],
        out_specs=pl.BlockSpec((None, tb, RET_W), lambda i, j: (i, j, 0)),
        out_shape=jax.ShapeDtypeStruct((b, s, RET_W), F32),
        scratch_shapes=[pltpu.VMEM((N_PAIRS, LANES, LANES), F32)],
        compiler_params=_cparams(2), name="retention")(p_ret.reshape(b, s, RET_SLAB), cos_t, sin_t, dmat, qw, kw, gch)


def _retention_consts(s):
    half = HEAD_DIM // 2
    inv = 1.0 / (ROPE_BASE ** (jnp.arange(0, HEAD_DIM, 2, dtype=F32) / HEAD_DIM))
    ang = jnp.arange(s, dtype=F32)[:, None] * inv[None, :]
    cos, sin = jnp.cos(ang), jnp.sin(ang)
    cos_t = jnp.tile(cos, (1, LANES // half))
    sin_t = jnp.tile(jnp.concatenate([-sin, sin], axis=-1), (1, LANES // HEAD_DIM))
    c = RET_CHUNK
    log_g = jnp.log1p(-jnp.exp2(-5.0 - jnp.arange(RET_H, dtype=F32)))
    pos = jnp.arange(c, dtype=F32)
    diff = pos[:, None] - pos[None, :]
    causal = diff >= 0
    dmat = jnp.where(causal[None], jnp.exp(jnp.where(causal, diff, 0.0)[None] * log_g[:, None, None]), 0.0)
    lane_g = jnp.repeat(log_g, HEAD_DIM)[None, :]
    qw = jnp.exp((pos + 1.0)[:, None] * lane_g)
    kw = jnp.exp((c - 1.0 - pos)[:, None] * lane_g)
    gch = jnp.exp(c * lane_g)
    return cos_t, sin_t, dmat, qw, kw, gch


def _rwkv_kernel(p_ref, mu_ref, wwa_ref, w0_ref, a0_ref, g2_ref, kk_ref, ka_ref, rk_ref, gng_ref, gnb_ref,
                 o_ref, st_ref, carry_ref, r_s, ld_s, k_s, v_s, al_s, b_s, y_s):
    tb = p_ref.shape[0]
    c = RWKV_CHUNK
    w = RWKV_W

    @pl.when(pl.program_id(1) == 0)
    def _():
        st_ref[...] = jnp.zeros(st_ref.shape, F32)
        carry_ref[...] = jnp.zeros(carry_ref.shape, F32)

    m_a, row, col, same_head = _pair_masks()
    m_b = jnp.logical_not(m_a)

    p = p_ref[...]
    rowi = lax.broadcasted_iota(I32, (tb, 1), 0)
    prev = jnp.where(rowi == 0, carry_ref[...], pltpu.roll(p, 1, 0))
    carry_ref[...] = p[tb - 1:tb, :]
    z = p + (prev - p) * mu_ref[...]
    r = z[:, 0:w]
    kr = z[:, w:2 * w]
    vr = z[:, 2 * w:3 * w]
    wa = z[:, 3 * w:3 * w + LANES]
    gl = z[:, 3 * w + LANES:]
    lane = lax.broadcasted_iota(I32, (1, LANES), 1)
    pre = _dot_hi(jnp.where(lane < RWKV_W_LORA, jnp.tanh(wa), wa), wwa_ref[...])
    w_log = -_softplus(-(w0_ref[...] + pre[:, :w])) - 0.5
    a = _sigmoid(a0_ref[...] + pre[:, w:])
    g_rw = _dot_hi(_sigmoid(gl), g2_ref[...])
    hsum = jnp.where((lax.broadcasted_iota(I32, (w, w), 0) >> HEAD_SHIFT)
                     == (lax.broadcasted_iota(I32, (w, w), 1) >> HEAD_SHIFT), 1.0, 0.0).astype(F32)
    kk = kr * kk_ref[...]
    kk = kk / jnp.maximum(jnp.sqrt(_dot_hi(kk * kk, hsum)), 1e-12)
    kmod = kr * (1.0 + (a - 1.0) * ka_ref[...])
    bonus = _dot_hi(r * kmod * rk_ref[...], hsum) * vr
    r_s[...] = r
    ld_s[...] = -jnp.exp(w_log)
    k_s[...] = kmod
    v_s[...] = vr
    al_s[...] = -kk
    b_s[...] = kk * a

    tri_incl = row >= col
    tri_strict = row > col
    eye = row == col
    tri_c = (lax.broadcasted_iota(I32, (c, c), 0) >= lax.broadcasted_iota(I32, (c, c), 1)).astype(F32)
    levels = []
    for sh in range(c.bit_length() - 1):
        levels.append(((row >> (sh + 1)) == (col >> (sh + 1))) & (((row >> sh) & 1) == 1) & (((col >> sh) & 1) == 0))

    def stack(x):
        return jnp.concatenate([jnp.where(m_a, x, 0.0), jnp.where(m_b, x, 0.0)], axis=0)

    def chunk(ci, carry):
        rows = pl.ds(pl.multiple_of(ci * c, c), c)
        ld = ld_s[rows, :]
        lc = _dot_hi(tri_c, ld)
        p_in = jnp.exp(lc)
        p_ex = jnp.exp(lc - ld)
        p_inv = jnp.exp(-lc)
        p_end = jnp.exp(lc[c - 1:c, :])
        al_t = al_s[rows, :] * p_ex
        r_t = r_s[rows, :] * p_in
        b_t = b_s[rows, :] * p_inv
        k_t = k_s[rows, :] * p_inv
        vv = v_s[rows, :]
        ys = []
        for pi in range(N_PAIRS):
            lanes = slice(pi * LANES, (pi + 1) * LANES)
            la, lr = stack(al_t[:, lanes]), stack(r_t[:, lanes])
            rb, rk, vs = stack(b_t[:, lanes]), stack(k_t[:, lanes]), stack(vv[:, lanes])
            pe = p_end[:, lanes]
            x_ab = jnp.where(tri_strict, _dot_nt_hi(la, rb), 0.0)
            x_ak = jnp.where(tri_strict, _dot_nt_hi(la, rk), 0.0)
            x_rb = jnp.where(tri_incl, _dot_nt_hi(lr, rb), 0.0)
            x_rk = jnp.where(tri_incl, _dot_nt_hi(lr, rk), 0.0)
            inv = jnp.where(eye, 1.0, 0.0) + jnp.where(levels[0], x_ab, 0.0)
            for lv in levels[1:]:
                inv = inv + _dot_hi(_dot_hi(inv, jnp.where(lv, x_ab, 0.0)), inv)
            w0 = _dot_hi(x_ak, vs)
            aw = _dot_hi(inv, jnp.concatenate([la, w0], axis=1))
            ba = _dot_tn_hi(rb * pe, aw)
            kv = _dot_tn_hi(rk * pe, vs)
            m_mat = jnp.where(eye, pe, 0.0) + ba[:, :LANES]
            g_mat = ba[:, LANES:] + kv
            ra = _dot_hi(x_rb, aw)
            r_hat = lr + ra[:, :LANES]
            y0 = ra[:, LANES:] + _dot_hi(x_rk, vs)
            t0 = st_ref[pi]
            yst = _dot_hi(r_hat, t0) + y0
            st_ref[pi] = _dot_hi(m_mat, t0) + g_mat
            ys.append(yst[:c] + yst[c:])
        y_s[rows, :] = jnp.concatenate(ys, axis=1)
        return carry

    lax.fori_loop(0, tb // c, chunk, 0)

    y = y_s[...]
    outs = []
    for pi in range(N_PAIRS):
        outs.append(_head_norm_pair(y[:, pi * LANES:(pi + 1) * LANES], same_head, RWKV_GN_EPS))
    yn = jnp.concatenate(outs, axis=1) * gng_ref[...] + gnb_ref[...]
    o_ref[...] = (yn + bonus) * g_rw


def _rwkv(p_rwkv, params, b, s):
    tb = min(256, s)
    full = lambda a: pl.BlockSpec(a.shape, lambda i, j: (0,) * a.ndim)
    blk = lambda: pltpu.VMEM((tb, RWKV_W), F32)
    return pl.pallas_call(
        _rwkv_kernel, grid=(b, s // tb),
        in_specs=[pl.BlockSpec((None, tb, RWKV_SLAB), lambda i, j: (i, j, 0))] + [full(a) for a in params],
        out_specs=pl.BlockSpec((None, tb, RWKV_W), lambda i, j: (i, j, 0)),
        out_shape=jax.ShapeDtypeStruct((b, s, RWKV_W), F32),
        scratch_shapes=[pltpu.VMEM((N_PAIRS, LANES, LANES), F32), pltpu.VMEM((1, RWKV_SLAB), F32)] + [blk() for _ in range(7)],
        compiler_params=_cparams(2), name="rwkv7")(p_rwkv.reshape(b, s, RWKV_SLAB), *params)


def _rwkv_params(mu, w0, w2, a0, a2, g2, k_k, k_a, r_k, gn_g, gn_b):
    row = lambda v: v.reshape(1, -1)
    z = jnp.zeros((RWKV_W_LORA, RWKV_W), F32)
    wwa = jnp.concatenate([jnp.concatenate([w2, z], axis=1), jnp.concatenate([z, a2], axis=1)], axis=0)
    return (row(mu), wwa, row(w0), row(a0), g2, row(k_k), row(k_a), row(r_k), row(gn_g), row(gn_b))


def _lru_kernel(p_ref, cw_ref, cb_ref, wax_ref, bax_ref, lam_ref, o_ref, xs_ref, h_ref):
    tb = p_ref.shape[0]
    w = LRU_W

    @pl.when(pl.program_id(1) == 0)
    def _():
        xs_ref[0:SUBLANES, :] = jnp.zeros((SUBLANES, w), F32)
        h_ref[...] = jnp.zeros(h_ref.shape, F32)

    x = p_ref[:, 0:w]
    gate = p_ref[:, w:]
    xs_ref[SUBLANES:, :] = x
    xl = x * cw_ref[LRU_CONV - 1:LRU_CONV, :] + cb_ref[...]
    for d in range(1, LRU_CONV):
        xl = xl + xs_ref[pl.ds(SUBLANES - d, tb), :] * cw_ref[LRU_CONV - 1 - d:LRU_CONV - d, :]
    xs_ref[0:SUBLANES, :] = x[tb - SUBLANES:, :]
    gates = _dot_hi(xl, wax_ref[...]) + bax_ref[...]
    r_gate = _sigmoid(gates[:, :w])
    i_gate = _sigmoid(gates[:, w:])
    log_a = -LRU_C * r_gate * _softplus(-lam_ref[...])
    a = jnp.exp(log_a)
    bv = jnp.sqrt((1.0 - a) * (1.0 + a)) * (i_gate * xl)
    rowi = lax.broadcasted_iota(I32, (tb, 1), 0)
    d = 1
    while d < tb:
        keep = rowi >= d
        a_sh = jnp.where(keep, pltpu.roll(a, d, 0), 1.0)
        b_sh = jnp.where(keep, pltpu.roll(bv, d, 0), 0.0)
        bv = a * b_sh + bv
        a = a * a_sh
        d *= 2
    h = bv + a * h_ref[...]
    h_ref[...] = h[tb - 1:tb, :]
    o_ref[...] = h * _gelu_tanh(gate)


def _lru(p_lru, params, b, s):
    tb = min(512, s)
    full = lambda a: pl.BlockSpec(a.shape, lambda i, j: (0,) * a.ndim)
    return pl.pallas_call(
        _lru_kernel, grid=(b, s // tb),
        in_specs=[pl.BlockSpec((None, tb, LRU_SLAB), lambda i, j: (i, j, 0))] + [full(a) for a in params],
        out_specs=pl.BlockSpec((None, tb, LRU_W), lambda i, j: (i, j, 0)),
        out_shape=jax.ShapeDtypeStruct((b, s, LRU_W), F32),
        scratch_shapes=[pltpu.VMEM((tb + SUBLANES, LRU_W), F32), pltpu.VMEM((1, LRU_W), F32)],
        compiler_params=_cparams(2), name="rglru")(p_lru.reshape(b, s, LRU_SLAB), *params)


def _lru_params(conv_w, conv_b, wa, ba, wx, bx, lam):
    row = lambda v: v.reshape(1, -1)
    def bd(m):
        out = jnp.zeros((LRU_W, LRU_W), F32)
        for i in range(LRU_H):
            out = out.at[i * HEAD_DIM:(i + 1) * HEAD_DIM, i * HEAD_DIM:(i + 1) * HEAD_DIM].set(m[i])
        return out
    wax = jnp.concatenate([bd(wa), bd(wx)], axis=1)
    return (conv_w, row(conv_b), wax, jnp.concatenate([row(ba), row(bx)], axis=1), row(lam))


def _outproj_kernel(yr_ref, yw_ref, yl_ref, h_ref, w_ref, g_ref, b_ref, o_ref):
    ycat = jnp.concatenate([yr_ref[...], yw_ref[...], yl_ref[...]], axis=-1).astype(BF16)
    mix = jnp.dot(ycat, w_ref[...], preferred_element_type=F32)
    o_ref[...] = _layer_norm(ALPHA * h_ref[...] + mix, g_ref[...], b_ref[...])


def _out_projection(y_ret, y_rwkv, y_lru, h, w_bf, g, b):
    t = h.shape[0]
    tm = min(512, t)
    row = lambda w: pl.BlockSpec((tm, w), lambda i: (i, 0))
    full = lambda a: pl.BlockSpec(a.shape, lambda i: (0,) * a.ndim)
    return pl.pallas_call(
        _outproj_kernel, grid=(t // tm,),
        in_specs=[row(RET_W), row(RWKV_W), row(LRU_W), row(D_MODEL), full(w_bf), full(g), full(b)],
        out_specs=row(D_MODEL), out_shape=jax.ShapeDtypeStruct((t, D_MODEL), F32),
        compiler_params=_cparams(1), name="out_proj_ln")(y_ret, y_rwkv, y_lru, h, w_bf, g, b)


def _router_kernel(h_ref, w_ref, b_ref, ew_ref, ei_ref):
    logits = _dot_hi(h_ref[...], w_ref[...]) + b_ref[...]
    lane = lax.broadcasted_iota(I32, logits.shape, 1)
    neg = -jnp.inf
    gmask = lane < N_GROUPS
    gl = jnp.where(gmask, logits, neg)
    gmax = jnp.max(gl, -1, keepdims=True)
    g_sel = jnp.min(jnp.where(gl == gmax, lane, LANES), -1, keepdims=True)
    g_gate = 1.0 / jnp.sum(jnp.where(gmask, jnp.exp(gl - gmax), 0.0), -1, keepdims=True)
    emask = (lane >= N_GROUPS) & (lane < N_GROUPS + N_EXPERTS) & (((lane - N_GROUPS) >> GROUP_SHIFT) == g_sel)
    el = jnp.where(emask, logits, neg)
    emax = jnp.max(el, -1, keepdims=True)
    ee = jnp.where(emask, jnp.exp(el - emax), 0.0)
    pe = ee / jnp.sum(ee, -1, keepdims=True)
    pe = jnp.where(emask, pe, -1.0)
    p1 = jnp.max(pe, -1, keepdims=True)
    i1 = jnp.min(jnp.where(pe == p1, lane, LANES), -1, keepdims=True)
    pe2 = jnp.where(lane == i1, -1.0, pe)
    p2 = jnp.max(pe2, -1, keepdims=True)
    i2 = jnp.min(jnp.where(pe2 == p2, lane, LANES), -1, keepdims=True)
    tot = p1 + p2
    ew_ref[...] = jnp.where(lane == 0, g_gate * (p1 / tot), jnp.where(lane == 1, g_gate * (p2 / tot), 0.0))
    ei_ref[...] = jnp.where(lane == 0, i1 - N_GROUPS, jnp.where(lane == 1, i2 - N_GROUPS, 0))


def _router(h, w_r, b_r):
    t = h.shape[0]
    tm = min(512, t)
    row = lambda w: pl.BlockSpec((tm, w), lambda i: (i, 0))
    full = lambda a: pl.BlockSpec(a.shape, lambda i: (0,) * a.ndim)
    return pl.pallas_call(
        _router_kernel, grid=(t // tm,), in_specs=[row(D_MODEL), full(w_r), full(b_r)],
        out_specs=[row(LANES), row(LANES)],
        out_shape=[jax.ShapeDtypeStruct((t, LANES), F32), jax.ShapeDtypeStruct((t, LANES), I32)],
        compiler_params=_cparams(1), name="router")(h, w_r, b_r)


def _row_copy(src, dst, sem, s_row, d_row):
    return pltpu.make_async_copy(src.at[pl.ds(s_row, 1)], dst.at[pl.ds(d_row, 1)], sem)


def _dispatch_kernel(dest_ref, h_ref, xs_in_ref, xs_ref, sem, *, td):
    del xs_in_ref
    base = pl.program_id(0) * td

    def start(i, c):
        tok = base + i
        _row_copy(h_ref, xs_ref, sem, tok, dest_ref[2 * tok]).start()
        _row_copy(h_ref, xs_ref, sem, tok, dest_ref[2 * tok + 1]).start()
        return c

    def wait(i, c):
        tok = base + i
        _row_copy(h_ref, xs_ref, sem, tok, dest_ref[2 * tok]).wait()
        _row_copy(h_ref, xs_ref, sem, tok, dest_ref[2 * tok + 1]).wait()
        return c

    lax.fori_loop(0, td, start, 0)
    lax.fori_loop(0, td, wait, 0)


def _dispatch(h, dest, n_slots):
    t = h.shape[0]
    td = min(512, t)
    xs0 = jnp.zeros((n_slots, D_MODEL), F32)
    any_spec = pl.BlockSpec(memory_space=pl.ANY)
    return pl.pallas_call(
        functools.partial(_dispatch_kernel, td=td),
        grid_spec=pltpu.PrefetchScalarGridSpec(
            num_scalar_prefetch=1, grid=(t // td,), in_specs=[any_spec, any_spec], out_specs=any_spec,
            scratch_shapes=[pltpu.SemaphoreType.DMA(())]),
        out_shape=jax.ShapeDtypeStruct((n_slots, D_MODEL), F32),
        input_output_aliases={2: 0},
        compiler_params=_cparams(1), name="moe_dispatch")(dest, h, xs0)


def _expert_kernel(be_ref, nu_ref, x_ref, wg_ref, wu_ref, wd_ref, o_ref):
    del be_ref
    i = pl.program_id(0)

    @pl.when(i < nu_ref[0])
    def _():
        x = x_ref[...].astype(BF16)
        hid = _silu(jnp.dot(x, wg_ref[...], preferred_element_type=F32)) * jnp.dot(x, wu_ref[...], preferred_element_type=F32)
        o_ref[...] = jnp.dot(hid.astype(BF16), wd_ref[...], preferred_element_type=F32)

    @pl.when(i >= nu_ref[0])
    def _():
        o_ref[...] = jnp.zeros(o_ref.shape, F32)


def _experts(xs, block_exp, n_used, wg, wu, wd):
    n_blocks = xs.shape[0] // MOE_BLOCK
    return pl.pallas_call(
        _expert_kernel,
        grid_spec=pltpu.PrefetchScalarGridSpec(
            num_scalar_prefetch=2, grid=(n_blocks,),
            in_specs=[pl.BlockSpec((MOE_BLOCK, D_MODEL), lambda i, be, nu: (i, 0)),
                      pl.BlockSpec((None, D_MODEL, D_EXPERT), lambda i, be, nu: (be[i], 0, 0)),
                      pl.BlockSpec((None, D_MODEL, D_EXPERT), lambda i, be, nu: (be[i], 0, 0)),
                      pl.BlockSpec((None, D_EXPERT, D_MODEL), lambda i, be, nu: (be[i], 0, 0))],
            out_specs=pl.BlockSpec((MOE_BLOCK, D_MODEL), lambda i, be, nu: (i, 0))),
        out_shape=jax.ShapeDtypeStruct(xs.shape, F32),
        compiler_params=_cparams(1), name="moe_experts")(block_exp, n_used, xs, wg, wu, wd)


def _combine_kernel(dest_ref, yb_ref, ew_ref, h_ref, g_ref, b_ref, o_ref, buf, sems):
    tc = h_ref.shape[0]
    i = pl.program_id(0)
    n = pl.num_programs(0)

    def copies(step, slot, fn):
        def body(r, c):
            tok = step * tc + r
            for j in range(2):
                fn(pltpu.make_async_copy(yb_ref.at[pl.ds(dest_ref[2 * tok + j], 1)],
                                         buf.at[slot, j, pl.ds(r, 1)], sems.at[slot]))
            return c
        lax.fori_loop(0, tc, body, 0)

    @pl.when(i == 0)
    def _():
        copies(0, 0, lambda cp: cp.start())

    @pl.when(i + 1 < n)
    def _():
        copies(i + 1, (i + 1) & 1, lambda cp: cp.start())

    slot = i & 1
    copies(i, slot, lambda cp: cp.wait())
    ew = ew_ref[...]
    y = ew[:, 0:1] * buf[slot, 0] + ew[:, 1:2] * buf[slot, 1]
    o_ref[...] = _layer_norm(ALPHA * h_ref[...] + y, g_ref[...], b_ref[...])


def _combine(yb, dest, ew, h, g, b):
    t = h.shape[0]
    tc = min(256, t)
    row = lambda w: pl.BlockSpec((tc, w), lambda i, d: (i, 0))
    full = lambda a: pl.BlockSpec(a.shape, lambda i, d: (0,) * a.ndim)
    return pl.pallas_call(
        _combine_kernel,
        grid_spec=pltpu.PrefetchScalarGridSpec(
            num_scalar_prefetch=1, grid=(t // tc,),
            in_specs=[pl.BlockSpec(memory_space=pl.ANY), row(LANES), row(D_MODEL), full(g), full(b)],
            out_specs=row(D_MODEL),
            scratch_shapes=[pltpu.VMEM((2, 2, tc, D_MODEL), F32), pltpu.SemaphoreType.DMA((2,))]),
        out_shape=jax.ShapeDtypeStruct((t, D_MODEL), F32),
        compiler_params=_cparams(1), name="moe_combine_ln")(dest, yb, ew, h, g, b)


def _moe_plan(ei, t):
    a = t * 2
    e_flat = ei[:, :2].reshape(a)
    onehot = (e_flat[:, None] == jnp.arange(N_EXPERTS, dtype=I32)[None, :]).astype(I32)
    csum = jnp.cumsum(onehot, axis=0)
    rank = jnp.take_along_axis(csum, e_flat[:, None], axis=1)[:, 0] - 1
    counts = csum[-1]
    padded = ((counts + MOE_BLOCK - 1) // MOE_BLOCK) * MOE_BLOCK
    pends = jnp.cumsum(padded)
    pstarts = pends - padded
    dest = (pstarts[e_flat] + rank).astype(I32)
    n_blocks = -(-a // MOE_BLOCK) + N_EXPERTS
    block_exp = jnp.minimum(jnp.searchsorted(pends, jnp.arange(n_blocks) * MOE_BLOCK, side='right'), N_EXPERTS - 1).astype(I32)
    n_used = (pends[-1:] // MOE_BLOCK).astype(I32)
    return dest, block_exp, n_used, n_blocks * MOE_BLOCK


def _hier_moe_ln(h, w_r, b_r, wg, wu, wd, g, b):
    t = h.shape[0]
    ew, ei = _router(h, w_r, b_r)
    dest, block_exp, n_used, n_slots = _moe_plan(ei, t)
    xs = _dispatch(h, dest, n_slots)
    yb = _experts(xs, block_exp, n_used, wg, wu, wd)
    return _combine(yb, dest, ew, h, g, b)


def kernel(x, ln_in_g, ln_in_b, w_in, w_out, rwkv_mu, rwkv_w0, rwkv_w2, rwkv_a0, rwkv_a2, rwkv_g2, rwkv_k_k, rwkv_k_a, rwkv_r_k, rwkv_gn_g, rwkv_gn_b, lru_conv_w, lru_conv_b, lru_wa, lru_ba, lru_wx, lru_bx, lru_lambda, ln1_g, ln1_b, moe_wg, moe_bg, moe_we, moe_be, moe_w_gate, moe_w_up, moe_w_down, ln2_g, ln2_b):
    b, s, d = x.shape
    t = b * s
    row = lambda v: v.reshape(1, -1)
    ret_consts = _retention_consts(s)
    h = x.reshape(t, d)
    for l in range(DEPTH):
        w_in_bf = w_in[l].astype(BF16)
        if l == 0:
            h, p_ret, p_rwkv, p_lru = _in_projection(h, w_in_bf, ln=(row(ln_in_g), row(ln_in_b)))
        else:
            p_ret, p_rwkv, p_lru = _in_projection(h, w_in_bf)
        y_ret = _retention(p_ret, ret_consts, b, s)
        y_rwkv = _rwkv(p_rwkv, _rwkv_params(rwkv_mu[l], rwkv_w0[l], rwkv_w2[l], rwkv_a0[l], rwkv_a2[l], rwkv_g2[l],
                                            rwkv_k_k[l], rwkv_k_a[l], rwkv_r_k[l], rwkv_gn_g[l], rwkv_gn_b[l]), b, s)
        y_lru = _lru(p_lru, _lru_params(lru_conv_w[l], lru_conv_b[l], lru_wa[l], lru_ba[l], lru_wx[l], lru_bx[l],
                                        lru_lambda[l]), b, s)
        h = _out_projection(y_ret.reshape(t, RET_W), y_rwkv.reshape(t, RWKV_W), y_lru.reshape(t, LRU_W), h,
                            w_out[l].astype(BF16), row(ln1_g[l]), row(ln1_b[l]))
        pad = jnp.zeros((d, LANES - N_GROUPS - N_EXPERTS), F32)
        w_r = jnp.concatenate([moe_wg[l], moe_we[l], pad], axis=1)
        b_r = jnp.concatenate([moe_bg[l], moe_be[l], jnp.zeros((LANES - N_GROUPS - N_EXPERTS,), F32)]).reshape(1, LANES)
        h = _hier_moe_ln(h, w_r, b_r, moe_w_gate[l].astype(BF16), moe_w_up[l].astype(BF16), moe_w_down[l].astype(BF16),
                         row(ln2_g[l]), row(ln2_b[l]))
    return h.reshape(b, s, d)
```

```python
import functools

import jax
import jax.numpy as jnp
from jax import lax
from jax.experimental import pallas as pl
from jax.experimental.pallas import tpu as pltpu

F32 = jnp.float32
BF16 = jnp.bfloat16
I32 = jnp.int32
HIGHEST = lax.Precision.HIGHEST

D_MODEL = 1024
DEPTH = 2
HEAD_DIM = 64
HEAD_SHIFT = 6
RET_H = 6
RET_W = RET_H * HEAD_DIM
RET_CHUNK = 128
ROPE_BASE = 10000.0
RWKV_H = 6
RWKV_W = RWKV_H * HEAD_DIM
RWKV_W_LORA = 64
RWKV_A_LORA = 64
RWKV_G_LORA = 128
RWKV_GN_EPS = 64e-5
LRU_H = 4
LRU_W = LRU_H * HEAD_DIM
LRU_CONV = 4
LRU_C = 8.0
RET_SLAB = 4 * RET_W
RWKV_SLAB = 3 * RWKV_W + RWKV_W_LORA + RWKV_A_LORA + RWKV_G_LORA
LRU_SLAB = 2 * LRU_W
IN_W = RET_SLAB + RWKV_SLAB + LRU_SLAB
N_GROUPS = 4
EXPERTS_PER_GROUP = 8
GROUP_SHIFT = 3
N_EXPERTS = N_GROUPS * EXPERTS_PER_GROUP
D_EXPERT = D_MODEL // 2
MOE_BLOCK = 256
LN_EPS = 1e-5
ALPHA = (2.0 * DEPTH) ** 0.25

LANES = 128
SUBLANES = 8
N_PAIRS = RET_W // LANES
RWKV_CHUNK = 64
RWKV_CHUNK_SHIFT = 6
VMEM_LIMIT = 56 * 1024 * 1024


def _cparams(n_axes, vmem=VMEM_LIMIT):
    return pltpu.CompilerParams(dimension_semantics=("arbitrary",) * n_axes, vmem_limit_bytes=vmem)


def _dot_bf(a, b):
    return jnp.dot(a.astype(BF16), b.astype(BF16), preferred_element_type=F32)


def _dot_hi(a, b):
    return jnp.dot(a, b, precision=HIGHEST, preferred_element_type=F32)


def _split_dot(a, b, passes, split_lhs):
    rem = a if split_lhs else b
    acc = None
    for _ in range(passes):
        piece = rem.astype(BF16)
        part = jnp.dot(piece, b, preferred_element_type=F32) if split_lhs else jnp.dot(a, piece, preferred_element_type=F32)
        acc = part if acc is None else acc + part
        rem = rem - piece.astype(F32)
    return acc


def _bmm(a, b):
    return jnp.einsum('nij,njk->nik', a.astype(BF16), b.astype(BF16), preferred_element_type=F32)


def _sigmoid(x):
    return 1.0 / (1.0 + jnp.exp(-x))


def _silu(x):
    return x * _sigmoid(x)


def _softplus(x):
    return jnp.maximum(x, 0.0) + jnp.log1p(jnp.exp(-jnp.abs(x)))


def _gelu_tanh(x):
    return 0.5 * x * (1.0 + jnp.tanh(0.7978845608028654 * (x + 0.044715 * (x * x * x))))


def _layer_norm(x, g, b):
    mu = jnp.mean(x, -1, keepdims=True)
    xc = x - mu
    var = jnp.mean(xc * xc, -1, keepdims=True)
    return xc * lax.rsqrt(var + LN_EPS) * g + b


def _pair_masks():
    lane = lax.broadcasted_iota(I32, (1, LANES), 1)
    m_a = lane < HEAD_DIM
    row = lax.broadcasted_iota(I32, (LANES, LANES), 0)
    col = lax.broadcasted_iota(I32, (LANES, LANES), 1)
    same_head = (row >> HEAD_SHIFT) == (col >> HEAD_SHIFT)
    return m_a, row, col, same_head


def _head_norm_pair(y, same_head, eps):
    avg = jnp.where(same_head, 1.0 / HEAD_DIM, 0.0).astype(BF16)
    mu = _split_dot(y, avg, 2, True)
    yc = y - mu
    var = _split_dot(yc * yc, avg, 2, True)
    return yc * lax.rsqrt(var + eps)


def _inproj_ln_kernel(x_ref, g_ref, b_ref, w_ref, h_ref, pr_ref, pw_ref, pl_ref):
    h = _layer_norm(x_ref[...], g_ref[...], b_ref[...])
    h_ref[...] = h
    p = jnp.dot(h.astype(BF16), w_ref[...], preferred_element_type=F32)
    pr_ref[...] = p[:, :RET_SLAB]
    pw_ref[...] = p[:, RET_SLAB:RET_SLAB + RWKV_SLAB]
    pl_ref[...] = p[:, RET_SLAB + RWKV_SLAB:]


def _inproj_kernel(h_ref, w_ref, pr_ref, pw_ref, pl_ref):
    p = jnp.dot(h_ref[...].astype(BF16), w_ref[...], preferred_element_type=F32)
    pr_ref[...] = p[:, :RET_SLAB]
    pw_ref[...] = p[:, RET_SLAB:RET_SLAB + RWKV_SLAB]
    pl_ref[...] = p[:, RET_SLAB + RWKV_SLAB:]


def _in_projection(x2d, w_bf, ln=None):
    t = x2d.shape[0]
    tm = min(512, t)
    row = lambda w: pl.BlockSpec((tm, w), lambda i: (i, 0))
    full = lambda a: pl.BlockSpec(a.shape, lambda i: (0,) * a.ndim)
    outs = [jax.ShapeDtypeStruct((t, RET_SLAB), F32), jax.ShapeDtypeStruct((t, RWKV_SLAB), F32),
            jax.ShapeDtypeStruct((t, LRU_SLAB), F32)]
    out_specs = [row(RET_SLAB), row(RWKV_SLAB), row(LRU_SLAB)]
    if ln is None:
        return pl.pallas_call(
            _inproj_kernel, grid=(t // tm,), in_specs=[row(D_MODEL), full(w_bf)], out_specs=out_specs,
            out_shape=outs, compiler_params=_cparams(1), name="in_proj")(x2d, w_bf)
    g, b = ln
    return pl.pallas_call(
        _inproj_ln_kernel, grid=(t // tm,), in_specs=[row(D_MODEL), full(g), full(b), full(w_bf)],
        out_specs=[row(D_MODEL)] + out_specs, out_shape=[jax.ShapeDtypeStruct((t, D_MODEL), F32)] + outs,
        compiler_params=_cparams(1), name="ln_in_proj")(x2d, g, b, w_bf)


def _swap_halves(t):
    n = t.shape[-1]
    lane = lax.broadcasted_iota(I32, (1, n), 1)
    lower = (lane & (HEAD_DIM - 1)) < (HEAD_DIM // 2)
    return jnp.where(lower, pltpu.roll(t, n - HEAD_DIM // 2, 1), pltpu.roll(t, HEAD_DIM // 2, 1))


def _retention_kernel(p_ref, cos_ref, sin_ref, dmat_ref, qw_ref, kw_ref, gch_ref, o_ref, st_ref):
    @pl.when(pl.program_id(1) == 0)
    def _():
        st_ref[...] = jnp.zeros(st_ref.shape, F32)

    tb = p_ref.shape[0]
    m_a, _, _, same_head = _pair_masks()
    cos = jnp.concatenate([cos_ref[...]] * N_PAIRS, axis=-1)
    sin = jnp.concatenate([sin_ref[...]] * N_PAIRS, axis=-1)
    q = p_ref[:, 0:RET_W]
    k = p_ref[:, RET_W:2 * RET_W]
    q = q * cos + _swap_halves(q) * sin
    k = (k * cos + _swap_halves(k) * sin) * (HEAD_DIM ** -0.5)
    c = RET_CHUNK
    for ci in range(tb // c):
        rows = slice(ci * c, (ci + 1) * c)
        for p in range(N_PAIRS):
            lanes = slice(p * LANES, (p + 1) * LANES)
            qp, kp = q[rows, lanes], k[rows, lanes]
            vp = p_ref[rows, 2 * RET_W + p * LANES:2 * RET_W + (p + 1) * LANES]
            gp = p_ref[rows, 3 * RET_W + p * LANES:3 * RET_W + (p + 1) * LANES]
            kb = kp.astype(BF16)
            intra = jnp.zeros((c, LANES), F32)
            for h in range(2):
                m = m_a if h == 0 else jnp.logical_not(m_a)
                s = lax.dot_general(jnp.where(m, qp, 0.0).astype(BF16), kb, (((1,), (1,)), ((), ())),
                                    preferred_element_type=F32)
                s = s * dmat_ref[2 * p + h]
                intra = intra + _dot_bf(s, jnp.where(m, vp, 0.0))
            st = st_ref[p]
            cross = _dot_bf(qp * qw_ref[:, lanes], st)
            kv = jnp.dot((kp * kw_ref[:, lanes]).astype(BF16).T, vp.astype(BF16), preferred_element_type=F32)
            st_ref[p] = st * gch_ref[:, lanes] + jnp.where(same_head, kv, 0.0)
            y = _head_norm_pair(intra + cross, same_head, LN_EPS)
            o_ref[rows, lanes] = y * _silu(gp)


def _retention(p_ret, consts, b, s):
    tb = min(512, s)
    cos_t, sin_t, dmat, qw, kw, gch = consts
    full = lambda a: pl.BlockSpec(a.shape, lambda i, j: (0,) * a.ndim)
    return pl.pallas_call(
        _retention_kernel, grid=(b, s // tb),
        in_specs=[pl.BlockSpec((None, tb, RET_SLAB), lambda i, j: (i, j, 0)),
                  pl.BlockSpec((tb, LANES), lambda i, j: (j, 0)), pl.BlockSpec((tb, LANES), lambda i, j: (j, 0)),
                  full(dmat), full(qw), full(kw), full(gch)],
        out_specs=pl.BlockSpec((None, tb, RET_W), lambda i, j: (i, j, 0)),
        out_shape=jax.ShapeDtypeStruct((b, s, RET_W), F32),
        scratch_shapes=[pltpu.VMEM((N_PAIRS, LANES, LANES), F32)],
        compiler_params=_cparams(2), name="retention")(p_ret.reshape(b, s, RET_SLAB), cos_t, sin_t, dmat, qw, kw, gch)


def _retention_consts(s):
    half = HEAD_DIM // 2
    inv = 1.0 / (ROPE_BASE ** (jnp.arange(0, HEAD_DIM, 2, dtype=F32) / HEAD_DIM))
    ang = jnp.arange(s, dtype=F32)[:, None] * inv[None, :]
    cos, sin = jnp.cos(ang), jnp.sin(ang)
    cos_t = jnp.tile(cos, (1, LANES // half))
    sin_t = jnp.tile(jnp.concatenate([-sin, sin], axis=-1), (1, LANES // HEAD_DIM))
    c = RET_CHUNK
    log_g = jnp.log1p(-jnp.exp2(-5.0 - jnp.arange(RET_H, dtype=F32)))
    pos = jnp.arange(c, dtype=F32)
    diff = pos[:, None] - pos[None, :]
    causal = diff >= 0
    dmat = jnp.where(causal[None], jnp.exp(jnp.where(causal, diff, 0.0)[None] * log_g[:, None, None]), 0.0)
    lane_g = jnp.repeat(log_g, HEAD_DIM)[None, :]
    qw = jnp.exp((pos + 1.0)[:, None] * lane_g)
    kw = jnp.exp((c - 1.0 - pos)[:, None] * lane_g)
    gch = jnp.exp(c * lane_g)
    return cos_t, sin_t, dmat, qw, kw, gch


def _rwkv_kernel(p_ref, mu_ref, wwa_ref, w0_ref, a0_ref, g2_ref, kk_ref, ka_ref, rk_ref, gng_ref, gnb_ref,
                 o_ref, st_ref, carry_ref, rm_s, g_s, y0_s):
    tb = p_ref.shape[0]
    c = RWKV_CHUNK
    w = RWKV_W
    n_ch = tb // c
    pairs = [slice(pi * LANES, (pi + 1) * LANES) for pi in range(N_PAIRS)]

    @pl.when(pl.program_id(1) == 0)
    def _():
        st_ref[...] = jnp.zeros(st_ref.shape, F32)
        carry_ref[...] = jnp.zeros(carry_ref.shape, F32)

    m_a, row, col, same_head = _pair_masks()
    m_b = jnp.logical_not(m_a)
    head_ones = jnp.where(same_head, 1.0, 0.0).astype(BF16)

    def head_sums(x):
        return jnp.concatenate([_split_dot(x[:, ln], head_ones, 2, True) for ln in pairs], axis=1)

    p = p_ref[...]
    rowi = lax.broadcasted_iota(I32, (tb, 1), 0)
    prev = jnp.where(rowi == 0, carry_ref[...], pltpu.roll(p, 1, 0))
    carry_ref[...] = p[tb - 1:tb, :]
    z = p + (prev - p) * mu_ref[...]
    r = z[:, 0:w]
    kr = z[:, w:2 * w]
    vr = z[:, 2 * w:3 * w]
    wa = z[:, 3 * w:3 * w + LANES]
    gl = z[:, 3 * w + LANES:]
    lane = lax.broadcasted_iota(I32, (1, LANES), 1)
    pre = _dot_bf(jnp.where(lane < RWKV_W_LORA, jnp.tanh(wa), wa), wwa_ref[...])
    w_log = -_softplus(-(w0_ref[...] + pre[:, :w])) - 0.5
    a = _sigmoid(a0_ref[...] + pre[:, w:])
    g_rw = _dot_bf(_sigmoid(gl), g2_ref[...])
    kk = kr * kk_ref[...]
    kk = kk / jnp.maximum(jnp.sqrt(head_sums(kk * kk)), 1e-12)
    kmod = kr * (1.0 + (a - 1.0) * ka_ref[...])
    bonus = head_sums(r * kmod * rk_ref[...]) * vr
    ld = -jnp.exp(w_log)

    t_r = lax.broadcasted_iota(I32, (tb, tb), 0)
    t_c = lax.broadcasted_iota(I32, (tb, tb), 1)
    same_chunk = (t_r >> RWKV_CHUNK_SHIFT) == (t_c >> RWKV_CHUNK_SHIFT)
    lc = _split_dot(jnp.where(same_chunk & (t_r >= t_c), 1.0, 0.0).astype(BF16), ld, 3, False)
    lc_end = _split_dot(jnp.where(same_chunk, 1.0, 0.0).astype(BF16), ld, 3, False)
    p_inv = jnp.exp(-lc)
    p_rem = jnp.exp(lc_end - lc)
    p_end = jnp.exp(lc_end)
    b0 = kk * a

    def batch(x):
        return jnp.stack([jnp.concatenate([jnp.where(m_a, x[ci * c:(ci + 1) * c, ln], 0.0),
                                           jnp.where(m_b, x[ci * c:(ci + 1) * c, ln], 0.0)], axis=0)
                          for ci in range(n_ch) for ln in pairs])

    lr = batch(r * jnp.exp(lc))
    la_b = batch(-kk * jnp.exp(lc - ld)).astype(BF16)
    lr_b = lr.astype(BF16)
    rb_b = batch(b0 * p_inv).astype(BF16)
    rk_b = batch(kmod * p_inv).astype(BF16)
    vs_b = batch(vr).astype(BF16)
    be_t = jnp.swapaxes(batch(b0 * p_rem), 1, 2)
    ke_t = jnp.swapaxes(batch(kmod * p_rem), 1, 2)
    pe = jnp.stack([p_end[ci * c:ci * c + 1, ln] for ci in range(n_ch) for ln in pairs])

    tri_incl = (row >= col)[None]
    tri_strict = (row > col)[None]
    eye = (row == col)[None]
    xx = jnp.einsum('nik,njk->nij', jnp.concatenate([la_b, lr_b], axis=1), jnp.concatenate([rb_b, rk_b], axis=1),
                    preferred_element_type=F32)
    x_ab = jnp.where(tri_strict, xx[:, :LANES, :LANES], 0.0)
    x_ak = jnp.where(tri_strict, xx[:, :LANES, LANES:], 0.0)
    x_rb = jnp.where(tri_incl, xx[:, LANES:, :LANES], 0.0)
    x_rk = jnp.where(tri_incl, xx[:, LANES:, LANES:], 0.0)
    inv = None
    for sh in range(RWKV_CHUNK_SHIFT):
        lv = (((row >> (sh + 1)) == (col >> (sh + 1))) & (((row >> sh) & 1) == 1) & (((col >> sh) & 1) == 0))[None]
        x_lv = jnp.where(lv, x_ab, 0.0)
        inv = jnp.where(eye, 1.0, 0.0) + x_lv if inv is None else inv + _bmm(_bmm(inv, x_lv), inv)
    wy = _bmm(jnp.concatenate([x_ak, x_rk], axis=1), vs_b)
    aw_b = _bmm(inv, jnp.concatenate([la_b, wy[:, :LANES].astype(BF16)], axis=2)).astype(BF16)
    ba = _bmm(be_t, aw_b)
    m_mat = jnp.where(eye, pe, 0.0) + ba[:, :, :LANES]
    ra = _bmm(x_rb, aw_b)
    rm_s[...] = jnp.concatenate([lr + ra[:, :, :LANES], m_mat], axis=1).astype(BF16)
    g_s[...] = ba[:, :, LANES:] + _bmm(ke_t, vs_b)
    y0_s[...] = ra[:, :, LANES:] + wy[:, LANES:]

    st = [st_ref[pi] for pi in range(N_PAIRS)]
    y_rows = []
    for ci in range(n_ch):
        ys = []
        for pi in range(N_PAIRS):
            n = ci * N_PAIRS + pi
            sy = jnp.dot(rm_s[n], st[pi].astype(BF16), preferred_element_type=F32)
            yst = sy[:LANES] + y0_s[n]
            st[pi] = sy[LANES:] + g_s[n]
            ys.append(yst[:c] + yst[c:])
        y_rows.append(jnp.concatenate(ys, axis=1))
    for pi in range(N_PAIRS):
        st_ref[pi] = st[pi]
    y = jnp.concatenate(y_rows, axis=0)

    yn = jnp.concatenate([_head_norm_pair(y[:, ln], same_head, RWKV_GN_EPS) for ln in pairs], axis=1)
    o_ref[...] = (yn * gng_ref[...] + gnb_ref[...] + bonus) * g_rw


def _rwkv(p_rwkv, params, b, s):
    tb = min(256, s)
    n = (tb // RWKV_CHUNK) * N_PAIRS
    full = lambda a: pl.BlockSpec(a.shape, lambda i, j: (0,) * a.ndim)
    return pl.pallas_call(
        _rwkv_kernel, grid=(b, s // tb),
        in_specs=[pl.BlockSpec((None, tb, RWKV_SLAB), lambda i, j: (i, j, 0))] + [full(a) for a in params],
        out_specs=pl.BlockSpec((None, tb, RWKV_W), lambda i, j: (i, j, 0)),
        out_shape=jax.ShapeDtypeStruct((b, s, RWKV_W), F32),
        scratch_shapes=[pltpu.VMEM((N_PAIRS, LANES, LANES), F32), pltpu.VMEM((1, RWKV_SLAB), F32),
                        pltpu.VMEM((n, 2 * LANES, LANES), BF16), pltpu.VMEM((n, LANES, LANES), F32),
                        pltpu.VMEM((n, LANES, LANES), F32)],
        compiler_params=_cparams(2), name="rwkv7")(p_rwkv.reshape(b, s, RWKV_SLAB), *params)


def _rwkv_params(mu, w0, w2, a0, a2, g2, k_k, k_a, r_k, gn_g, gn_b):
    row = lambda v: v.reshape(1, -1)
    z = jnp.zeros((RWKV_W_LORA, RWKV_W), F32)
    wwa = jnp.concatenate([jnp.concatenate([w2, z], axis=1), jnp.concatenate([z, a2], axis=1)], axis=0)
    return (row(mu), wwa.astype(BF16), row(w0), row(a0), g2.astype(BF16), row(k_k), row(k_a), row(r_k), row(gn_g), row(gn_b))


def _lru_kernel(p_ref, cw_ref, cb_ref, wax_ref, bax_ref, lam_ref, o_ref, xs_ref, h_ref):
    tb = p_ref.shape[0]
    w = LRU_W

    @pl.when(pl.program_id(1) == 0)
    def _():
        xs_ref[0:SUBLANES, :] = jnp.zeros((SUBLANES, w), F32)
        h_ref[...] = jnp.zeros(h_ref.shape, F32)

    x = p_ref[:, 0:w]
    gate = p_ref[:, w:]
    xs_ref[SUBLANES:, :] = x
    xl = x * cw_ref[LRU_CONV - 1:LRU_CONV, :] + cb_ref[...]
    for d in range(1, LRU_CONV):
        xl = xl + xs_ref[pl.ds(SUBLANES - d, tb), :] * cw_ref[LRU_CONV - 1 - d:LRU_CONV - d, :]
    xs_ref[0:SUBLANES, :] = x[tb - SUBLANES:, :]
    gates = _dot_hi(xl, wax_ref[...]) + bax_ref[...]
    r_gate = _sigmoid(gates[:, :w])
    i_gate = _sigmoid(gates[:, w:])
    log_a = -LRU_C * r_gate * _softplus(-lam_ref[...])
    a = jnp.exp(log_a)
    bv = jnp.sqrt((1.0 - a) * (1.0 + a)) * (i_gate * xl)
    rowi = lax.broadcasted_iota(I32, (tb, 1), 0)
    d = 1
    while d < tb:
        keep = rowi >= d
        a_sh = jnp.where(keep, pltpu.roll(a, d, 0), 1.0)
        b_sh = jnp.where(keep, pltpu.roll(bv, d, 0), 0.0)
        bv = a * b_sh + bv
        a = a * a_sh
        d *= 2
    h = bv + a * h_ref[...]
    h_ref[...] = h[tb - 1:tb, :]
    o_ref[...] = h * _gelu_tanh(gate)


def _lru(p_lru, params, b, s):
    tb = min(512, s)
    full = lambda a: pl.BlockSpec(a.shape, lambda i, j: (0,) * a.ndim)
    return pl.pallas_call(
        _lru_kernel, grid=(b, s // tb),
        in_specs=[pl.BlockSpec((None, tb, LRU_SLAB), lambda i, j: (i, j, 0))] + [full(a) for a in params],
        out_specs=pl.BlockSpec((None, tb, LRU_W), lambda i, j: (i, j, 0)),
        out_shape=jax.ShapeDtypeStruct((b, s, LRU_W), F32),
        scratch_shapes=[pltpu.VMEM((tb + SUBLANES, LRU_W), F32), pltpu.VMEM((1, LRU_W), F32)],
        compiler_params=_cparams(2), name="rglru")(p_lru.reshape(b, s, LRU_SLAB), *params)


def _lru_params(conv_w, conv_b, wa, ba, wx, bx, lam):
    row = lambda v: v.reshape(1, -1)
    def bd(m):
        out = jnp.zeros((LRU_W, LRU_W), F32)
        for i in range(LRU_H):
            out = out.at[i * HEAD_DIM:(i + 1) * HEAD_DIM, i * HEAD_DIM:(i + 1) * HEAD_DIM].set(m[i])
        return out
    wax = jnp.concatenate([bd(wa), bd(wx)], axis=1)
    return (conv_w, row(conv_b), wax, jnp.concatenate([row(ba), row(bx)], axis=1), row(lam))


def _outproj_kernel(yr_ref, yw_ref, yl_ref, h_ref, w_ref, g_ref, b_ref, o_ref):
    ycat = jnp.concatenate([yr_ref[...], yw_ref[...], yl_ref[...]], axis=-1).astype(BF16)
    mix = jnp.dot(ycat, w_ref[...], preferred_element_type=F32)
    o_ref[...] = _layer_norm(ALPHA * h_ref[...] + mix, g_ref[...], b_ref[...])


def _out_projection(y_ret, y_rwkv, y_lru, h, w_bf, g, b):
    t = h.shape[0]
    tm = min(512, t)
    row = lambda w: pl.BlockSpec((tm, w), lambda i: (i, 0))
    full = lambda a: pl.BlockSpec(a.shape, lambda i: (0,) * a.ndim)
    return pl.pallas_call(
        _outproj_kernel, grid=(t // tm,),
        in_specs=[row(RET_W), row(RWKV_W), row(LRU_W), row(D_MODEL), full(w_bf), full(g), full(b)],
        out_specs=row(D_MODEL), out_shape=jax.ShapeDtypeStruct((t, D_MODEL), F32),
        compiler_params=_cparams(1), name="out_proj_ln")(y_ret, y_rwkv, y_lru, h, w_bf, g, b)


def _router_kernel(h_ref, w_ref, b_ref, ew_ref, ei_ref):
    logits = _dot_hi(h_ref[...], w_ref[...]) + b_ref[...]
    lane = lax.broadcasted_iota(I32, logits.shape, 1)
    neg = -jnp.inf
    gmask = lane < N_GROUPS
    gl = jnp.where(gmask, logits, neg)
    gmax = jnp.max(gl, -1, keepdims=True)
    g_sel = jnp.min(jnp.where(gl == gmax, lane, LANES), -1, keepdims=True)
    g_gate = 1.0 / jnp.sum(jnp.where(gmask, jnp.exp(gl - gmax), 0.0), -1, keepdims=True)
    emask = (lane >= N_GROUPS) & (lane < N_GROUPS + N_EXPERTS) & (((lane - N_GROUPS) >> GROUP_SHIFT) == g_sel)
    el = jnp.where(emask, logits, neg)
    emax = jnp.max(el, -1, keepdims=True)
    ee = jnp.where(emask, jnp.exp(el - emax), 0.0)
    pe = ee / jnp.sum(ee, -1, keepdims=True)
    pe = jnp.where(emask, pe, -1.0)
    p1 = jnp.max(pe, -1, keepdims=True)
    i1 = jnp.min(jnp.where(pe == p1, lane, LANES), -1, keepdims=True)
    pe2 = jnp.where(lane == i1, -1.0, pe)
    p2 = jnp.max(pe2, -1, keepdims=True)
    i2 = jnp.min(jnp.where(pe2 == p2, lane, LANES), -1, keepdims=True)
    tot = p1 + p2
    ew_ref[...] = jnp.where(lane == 0, g_gate * (p1 / tot), jnp.where(lane == 1, g_gate * (p2 / tot), 0.0))
    ei_ref[...] = jnp.where(lane == 0, i1 - N_GROUPS, jnp.where(lane == 1, i2 - N_GROUPS, 0))


def _router(h, w_r, b_r):
    t = h.shape[0]
    tm = min(512, t)
    row = lambda w: pl.BlockSpec((tm, w), lambda i: (i, 0))
    full = lambda a: pl.BlockSpec(a.shape, lambda i: (0,) * a.ndim)
    return pl.pallas_call(
        _router_kernel, grid=(t // tm,), in_specs=[row(D_MODEL), full(w_r), full(b_r)],
        out_specs=[row(LANES), row(LANES)],
        out_shape=[jax.ShapeDtypeStruct((t, LANES), F32), jax.ShapeDtypeStruct((t, LANES), I32)],
        compiler_params=_cparams(1), name="router")(h, w_r, b_r)


DMA_UNROLL = 8


def _dispatch_kernel(dest_ref, h_ref, xs_in_ref, xs_ref, sem):
    del xs_in_ref
    td = h_ref.shape[0]
    base = pl.program_id(0) * td

    def copy(r, j):
        return pltpu.make_async_copy(h_ref.at[pl.ds(r, 1)], xs_ref.at[pl.ds(dest_ref[2 * (base + r) + j], 1)], sem)

    def start(r, c):
        copy(r, 0).start()
        copy(r, 1).start()
        return c

    def wait(r, c):
        copy(r, 0).wait()
        copy(r, 1).wait()
        return c

    lax.fori_loop(0, td, start, 0, unroll=DMA_UNROLL)
    lax.fori_loop(0, td, wait, 0, unroll=DMA_UNROLL)


def _dispatch(h, dest, n_slots):
    t = h.shape[0]
    td = min(1024, t)
    xs0 = jnp.zeros((n_slots, D_MODEL), F32)
    any_spec = pl.BlockSpec(memory_space=pl.ANY)
    return pl.pallas_call(
        _dispatch_kernel,
        grid_spec=pltpu.PrefetchScalarGridSpec(
            num_scalar_prefetch=1, grid=(t // td,),
            in_specs=[pl.BlockSpec((td, D_MODEL), lambda i, d: (i, 0)), any_spec], out_specs=any_spec,
            scratch_shapes=[pltpu.SemaphoreType.DMA(())]),
        out_shape=jax.ShapeDtypeStruct((n_slots, D_MODEL), F32),
        input_output_aliases={2: 0},
        compiler_params=_cparams(1), name="moe_dispatch")(dest, h, xs0)


def _expert_kernel(be_ref, nu_ref, x_ref, wg_ref, wu_ref, wd_ref, o_ref):
    del be_ref
    i = pl.program_id(0)

    @pl.when(i < nu_ref[0])
    def _():
        x = x_ref[...].astype(BF16)
        hid = _silu(jnp.dot(x, wg_ref[...], preferred_element_type=F32)) * jnp.dot(x, wu_ref[...], preferred_element_type=F32)
        o_ref[...] = jnp.dot(hid.astype(BF16), wd_ref[...], preferred_element_type=F32)

    @pl.when(i >= nu_ref[0])
    def _():
        o_ref[...] = jnp.zeros(o_ref.shape, F32)


def _experts(xs, block_exp, n_used, wg, wu, wd):
    n_blocks = xs.shape[0] // MOE_BLOCK
    return pl.pallas_call(
        _expert_kernel,
        grid_spec=pltpu.PrefetchScalarGridSpec(
            num_scalar_prefetch=2, grid=(n_blocks,),
            in_specs=[pl.BlockSpec((MOE_BLOCK, D_MODEL), lambda i, be, nu: (i, 0)),
                      pl.BlockSpec((None, D_MODEL, D_EXPERT), lambda i, be, nu: (be[i], 0, 0)),
                      pl.BlockSpec((None, D_MODEL, D_EXPERT), lambda i, be, nu: (be[i], 0, 0)),
                      pl.BlockSpec((None, D_EXPERT, D_MODEL), lambda i, be, nu: (be[i], 0, 0))],
            out_specs=pl.BlockSpec((MOE_BLOCK, D_MODEL), lambda i, be, nu: (i, 0))),
        out_shape=jax.ShapeDtypeStruct(xs.shape, F32),
        compiler_params=_cparams(1), name="moe_experts")(block_exp, n_used, xs, wg, wu, wd)


def _combine_kernel(dest_ref, yb_ref, ew_ref, h_ref, g_ref, b_ref, o_ref, buf, sems):
    tc = h_ref.shape[0]
    i = pl.program_id(0)
    n = pl.num_programs(0)

    def copies(step, slot, fn):
        def body(r, c):
            tok = step * tc + r
            for j in range(2):
                fn(pltpu.make_async_copy(yb_ref.at[pl.ds(dest_ref[2 * tok + j], 1)],
                                         buf.at[slot, j, pl.ds(r, 1)], sems.at[slot]))
            return c
        lax.fori_loop(0, tc, body, 0, unroll=DMA_UNROLL)

    @pl.when(i == 0)
    def _():
        copies(0, 0, lambda cp: cp.start())

    @pl.when(i + 1 < n)
    def _():
        copies(i + 1, (i + 1) & 1, lambda cp: cp.start())

    slot = i & 1
    copies(i, slot, lambda cp: cp.wait())
    ew = ew_ref[...]
    y = ew[:, 0:1] * buf[slot, 0] + ew[:, 1:2] * buf[slot, 1]
    o_ref[...] = _layer_norm(ALPHA * h_ref[...] + y, g_ref[...], b_ref[...])


def _combine(yb, dest, ew, h, g, b):
    t = h.shape[0]
    tc = min(256, t)
    row = lambda w: pl.BlockSpec((tc, w), lambda i, d: (i, 0))
    full = lambda a: pl.BlockSpec(a.shape, lambda i, d: (0,) * a.ndim)
    return pl.pallas_call(
        _combine_kernel,
        grid_spec=pltpu.PrefetchScalarGridSpec(
            num_scalar_prefetch=1, grid=(t // tc,),
            in_specs=[pl.BlockSpec(memory_space=pl.ANY), row(LANES), row(D_MODEL), full(g), full(b)],
            out_specs=row(D_MODEL),
            scratch_shapes=[pltpu.VMEM((2, 2, tc, D_MODEL), F32), pltpu.SemaphoreType.DMA((2,))]),
        out_shape=jax.ShapeDtypeStruct((t, D_MODEL), F32),
        compiler_params=_cparams(1), name="moe_combine_ln")(dest, yb, ew, h, g, b)


def _moe_plan(ei, t):
    a = t * 2
    e_flat = ei[:, :2].reshape(a)
    onehot = (e_flat[:, None] == jnp.arange(N_EXPERTS, dtype=I32)[None, :]).astype(I32)
    csum = jnp.cumsum(onehot, axis=0)
    rank = jnp.take_along_axis(csum, e_flat[:, None], axis=1)[:, 0] - 1
    counts = csum[-1]
    padded = ((counts + MOE_BLOCK - 1) // MOE_BLOCK) * MOE_BLOCK
    pends = jnp.cumsum(padded)
    pstarts = pends - padded
    dest = (pstarts[e_flat] + rank).astype(I32)
    n_blocks = -(-a // MOE_BLOCK) + N_EXPERTS
    block_exp = jnp.minimum(jnp.searchsorted(pends, jnp.arange(n_blocks) * MOE_BLOCK, side='right'), N_EXPERTS - 1).astype(I32)
    n_used = (pends[-1:] // MOE_BLOCK).astype(I32)
    return dest, block_exp, n_used, n_blocks * MOE_BLOCK


def _hier_moe_ln(h, w_r, b_r, wg, wu, wd, g, b):
    t = h.shape[0]
    ew, ei = _router(h, w_r, b_r)
    dest, block_exp, n_used, n_slots = _moe_plan(ei, t)
    xs = _dispatch(h, dest, n_slots)
    yb = _experts(xs, block_exp, n_used, wg, wu, wd)
    return _combine(yb, dest, ew, h, g, b)


def kernel(x, ln_in_g, ln_in_b, w_in, w_out, rwkv_mu, rwkv_w0, rwkv_w2, rwkv_a0, rwkv_a2, rwkv_g2, rwkv_k_k, rwkv_k_a, rwkv_r_k, rwkv_gn_g, rwkv_gn_b, lru_conv_w, lru_conv_b, lru_wa, lru_ba, lru_wx, lru_bx, lru_lambda, ln1_g, ln1_b, moe_wg, moe_bg, moe_we, moe_be, moe_w_gate, moe_w_up, moe_w_down, ln2_g, ln2_b):
    b, s, d = x.shape
    t = b * s
    row = lambda v: v.reshape(1, -1)
    ret_consts = _retention_consts(s)
    h = x.reshape(t, d)
    for l in range(DEPTH):
        w_in_bf = w_in[l].astype(BF16)
        if l == 0:
            h, p_ret, p_rwkv, p_lru = _in_projection(h, w_in_bf, ln=(row(ln_in_g), row(ln_in_b)))
        else:
            p_ret, p_rwkv, p_lru = _in_projection(h, w_in_bf)
        y_ret = _retention(p_ret, ret_consts, b, s)
        y_rwkv = _rwkv(p_rwkv, _rwkv_params(rwkv_mu[l], rwkv_w0[l], rwkv_w2[l], rwkv_a0[l], rwkv_a2[l], rwkv_g2[l],
                                            rwkv_k_k[l], rwkv_k_a[l], rwkv_r_k[l], rwkv_gn_g[l], rwkv_gn_b[l]), b, s)
        y_lru = _lru(p_lru, _lru_params(lru_conv_w[l], lru_conv_b[l], lru_wa[l], lru_ba[l], lru_wx[l], lru_bx[l],
                                        lru_lambda[l]), b, s)
        h = _out_projection(y_ret.reshape(t, RET_W), y_rwkv.reshape(t, RWKV_W), y_lru.reshape(t, LRU_W), h,
                            w_out[l].astype(BF16), row(ln1_g[l]), row(ln1_b[l]))
        pad = jnp.zeros((d, LANES - N_GROUPS - N_EXPERTS), F32)
        w_r = jnp.concatenate([moe_wg[l], moe_we[l], pad], axis=1)
        b_r = jnp.concatenate([moe_bg[l], moe_be[l], jnp.zeros((LANES - N_GROUPS - N_EXPERTS,), F32)]).reshape(1, LANES)
        h = _hier_moe_ln(h, w_r, b_r, moe_w_gate[l].astype(BF16), moe_w_up[l].astype(BF16), moe_w_down[l].astype(BF16),
                         row(ln2_g[l]), row(ln2_b[l]))
    return h.reshape(b, s, d)
```

```python
import functools

import jax
import jax.numpy as jnp
from jax import lax
from jax.experimental import pallas as pl
from jax.experimental.pallas import tpu as pltpu

F32 = jnp.float32
BF16 = jnp.bfloat16
I32 = jnp.int32

D_MODEL = 1024
DEPTH = 2
HEAD_DIM = 64
HEAD_SHIFT = 6
RET_H = 6
RET_W = RET_H * HEAD_DIM
RET_CHUNK = 128
ROPE_BASE = 10000.0
RWKV_H = 6
RWKV_W = RWKV_H * HEAD_DIM
RWKV_W_LORA = 64
RWKV_A_LORA = 64
RWKV_G_LORA = 128
RWKV_GN_EPS = 64e-5
LRU_H = 4
LRU_W = LRU_H * HEAD_DIM
LRU_CONV = 4
LRU_C = 8.0
RET_SLAB = 4 * RET_W
RWKV_SLAB = 3 * RWKV_W + RWKV_W_LORA + RWKV_A_LORA + RWKV_G_LORA
LRU_SLAB = 2 * LRU_W
IN_W = RET_SLAB + RWKV_SLAB + LRU_SLAB
N_GROUPS = 4
EXPERTS_PER_GROUP = 8
GROUP_SHIFT = 3
N_EXPERTS = N_GROUPS * EXPERTS_PER_GROUP
D_EXPERT = D_MODEL // 2
MOE_BLOCK = 256
LN_EPS = 1e-5
ALPHA = (2.0 * DEPTH) ** 0.25

LANES = 128
SUBLANES = 8
N_PAIRS = RET_W // LANES
RWKV_CHUNK = 64
RWKV_CHUNK_SHIFT = 6
VMEM_LIMIT = 56 * 1024 * 1024


def _cparams(n_axes, vmem=VMEM_LIMIT):
    return pltpu.CompilerParams(dimension_semantics=("arbitrary",) * n_axes, vmem_limit_bytes=vmem)


def _dot_bf(a, b):
    return jnp.dot(a.astype(BF16), b.astype(BF16), preferred_element_type=F32)


def _split_dot(a, b, passes, split_lhs):
    rem = a if split_lhs else b
    acc = None
    for _ in range(passes):
        piece = rem.astype(BF16)
        part = jnp.dot(piece, b, preferred_element_type=F32) if split_lhs else jnp.dot(a, piece, preferred_element_type=F32)
        acc = part if acc is None else acc + part
        rem = rem - piece.astype(F32)
    return acc


def _bmm(a, b):
    return jnp.einsum('nij,njk->nik', a.astype(BF16), b.astype(BF16), preferred_element_type=F32)


def _sigmoid(x):
    return 1.0 / (1.0 + jnp.exp(-x))


def _silu(x):
    return x * _sigmoid(x)


def _softplus(x):
    return jnp.maximum(x, 0.0) + jnp.log1p(jnp.exp(-jnp.abs(x)))


def _gelu_tanh(x):
    return 0.5 * x * (1.0 + jnp.tanh(0.7978845608028654 * (x + 0.044715 * (x * x * x))))


def _layer_norm(x, g, b):
    mu = jnp.mean(x, -1, keepdims=True)
    xc = x - mu
    var = jnp.mean(xc * xc, -1, keepdims=True)
    return xc * lax.rsqrt(var + LN_EPS) * g + b


def _pair_masks():
    lane = lax.broadcasted_iota(I32, (1, LANES), 1)
    m_a = lane < HEAD_DIM
    row = lax.broadcasted_iota(I32, (LANES, LANES), 0)
    col = lax.broadcasted_iota(I32, (LANES, LANES), 1)
    same_head = (row >> HEAD_SHIFT) == (col >> HEAD_SHIFT)
    return m_a, row, col, same_head


def _head_norm_pair(y, same_head, eps):
    avg = jnp.where(same_head, 1.0 / HEAD_DIM, 0.0).astype(BF16)
    mu = _split_dot(y, avg, 2, True)
    yc = y - mu
    var = _split_dot(yc * yc, avg, 2, True)
    return yc * lax.rsqrt(var + eps)


def _inproj_ln_kernel(x_ref, g_ref, b_ref, w_ref, h_ref, pr_ref, pw_ref, pl_ref):
    h = _layer_norm(x_ref[...], g_ref[...], b_ref[...])
    h_ref[...] = h
    p = jnp.dot(h.astype(BF16), w_ref[...], preferred_element_type=F32)
    pr_ref[...] = p[:, :RET_SLAB]
    pw_ref[...] = p[:, RET_SLAB:RET_SLAB + RWKV_SLAB]
    pl_ref[...] = p[:, RET_SLAB + RWKV_SLAB:]


def _inproj_kernel(h_ref, w_ref, pr_ref, pw_ref, pl_ref):
    p = jnp.dot(h_ref[...].astype(BF16), w_ref[...], preferred_element_type=F32)
    pr_ref[...] = p[:, :RET_SLAB]
    pw_ref[...] = p[:, RET_SLAB:RET_SLAB + RWKV_SLAB]
    pl_ref[...] = p[:, RET_SLAB + RWKV_SLAB:]


def _in_projection(x2d, w_bf, ln=None):
    t = x2d.shape[0]
    tm = min(512, t)
    row = lambda w: pl.BlockSpec((tm, w), lambda i: (i, 0))
    full = lambda a: pl.BlockSpec(a.shape, lambda i: (0,) * a.ndim)
    outs = [jax.ShapeDtypeStruct((t, RET_SLAB), F32), jax.ShapeDtypeStruct((t, RWKV_SLAB), F32),
            jax.ShapeDtypeStruct((t, LRU_SLAB), F32)]
    out_specs = [row(RET_SLAB), row(RWKV_SLAB), row(LRU_SLAB)]
    if ln is None:
        return pl.pallas_call(
            _inproj_kernel, grid=(t // tm,), in_specs=[row(D_MODEL), full(w_bf)], out_specs=out_specs,
            out_shape=outs, compiler_params=_cparams(1), name="in_proj")(x2d, w_bf)
    g, b = ln
    return pl.pallas_call(
        _inproj_ln_kernel, grid=(t // tm,), in_specs=[row(D_MODEL), full(g), full(b), full(w_bf)],
        out_specs=[row(D_MODEL)] + out_specs, out_shape=[jax.ShapeDtypeStruct((t, D_MODEL), F32)] + outs,
        compiler_params=_cparams(1), name="ln_in_proj")(x2d, g, b, w_bf)


def _swap_halves(t):
    n = t.shape[-1]
    lane = lax.broadcasted_iota(I32, (1, n), 1)
    lower = (lane & (HEAD_DIM - 1)) < (HEAD_DIM // 2)
    return jnp.where(lower, pltpu.roll(t, n - HEAD_DIM // 2, 1), pltpu.roll(t, HEAD_DIM // 2, 1))


def _retention_kernel(p_ref, cos_ref, sin_ref, dmat_ref, qw_ref, kw_ref, gch_ref, o_ref, st_ref):
    @pl.when(pl.program_id(1) == 0)
    def _():
        st_ref[...] = jnp.zeros(st_ref.shape, F32)

    tb = p_ref.shape[0]
    c = RET_CHUNK
    n_ch = tb // c
    pairs = [slice(pi * LANES, (pi + 1) * LANES) for pi in range(N_PAIRS)]
    m_a, _, _, same_head = _pair_masks()
    m_b = jnp.logical_not(m_a)
    cos = jnp.concatenate([cos_ref[...]] * N_PAIRS, axis=-1)
    sin = jnp.concatenate([sin_ref[...]] * N_PAIRS, axis=-1)
    q = p_ref[:, 0:RET_W]
    k = p_ref[:, RET_W:2 * RET_W]
    v = p_ref[:, 2 * RET_W:3 * RET_W]
    q = q * cos + _swap_halves(q) * sin
    k = (k * cos + _swap_halves(k) * sin) * (HEAD_DIM ** -0.5)

    def per_problem(fn):
        return jnp.stack([fn(slice(ci * c, (ci + 1) * c), pi) for ci in range(n_ch) for pi in range(N_PAIRS)])

    def stack(x):
        return jnp.concatenate([jnp.where(m_a, x, 0.0), jnp.where(m_b, x, 0.0)], axis=0)

    k_b = per_problem(lambda rows, pi: k[rows, pairs[pi]]).astype(BF16)
    v_b = per_problem(lambda rows, pi: v[rows, pairs[pi]]).astype(BF16)
    q_st = per_problem(lambda rows, pi: stack(q[rows, pairs[pi]])).astype(BF16)
    v_st = per_problem(lambda rows, pi: stack(v[rows, pairs[pi]])).astype(BF16)
    q_w = per_problem(lambda rows, pi: q[rows, pairs[pi]] * qw_ref[:, pairs[pi]]).astype(BF16)
    kw_t = jnp.swapaxes(per_problem(lambda rows, pi: k[rows, pairs[pi]] * kw_ref[:, pairs[pi]]), 1, 2).astype(BF16)
    decay = per_problem(lambda rows, pi: jnp.concatenate([dmat_ref[2 * pi], dmat_ref[2 * pi + 1]], axis=0))

    s = jnp.einsum('nik,njk->nij', q_st, k_b, preferred_element_type=F32) * decay
    intra = _bmm(jnp.concatenate([s[:, :c], s[:, c:]], axis=2), v_st)
    kv = _bmm(kw_t, v_b)
    states = []
    for ci in range(n_ch):
        for pi in range(N_PAIRS):
            st = st_ref[pi]
            states.append(st)
            st_ref[pi] = st * gch_ref[:, pairs[pi]] + jnp.where(same_head, kv[ci * N_PAIRS + pi], 0.0)
    y_all = intra + _bmm(q_w, jnp.stack(states))
    y = jnp.concatenate([jnp.concatenate([y_all[ci * N_PAIRS + pi] for pi in range(N_PAIRS)], axis=1)
                         for ci in range(n_ch)], axis=0)
    yn = jnp.concatenate([_head_norm_pair(y[:, ln], same_head, LN_EPS) for ln in pairs], axis=1)
    o_ref[...] = yn * _silu(p_ref[:, 3 * RET_W:4 * RET_W])


def _retention(p_ret, consts, b, s):
    tb = min(512, s)
    cos_t, sin_t, dmat, qw, kw, gch = consts
    full = lambda a: pl.BlockSpec(a.shape, lambda i, j: (0,) * a.ndim)
    return pl.pallas_call(
        _retention_kernel, grid=(b, s // tb),
        in_specs=[pl.BlockSpec((None, tb, RET_SLAB), lambda i, j: (i, j, 0)),
                  pl.BlockSpec((tb, LANES), lambda i, j: (j, 0)), pl.BlockSpec((tb, LANES), lambda i, j: (j, 0)),
                  full(dmat), full(qw), full(kw), full(gch)],
        out_specs=pl.BlockSpec((None, tb, RET_W), lambda i, j: (i, j, 0)),
        out_shape=jax.ShapeDtypeStruct((b, s, RET_W), F32),
        scratch_shapes=[pltpu.VMEM((N_PAIRS, LANES, LANES), F32)],
        compiler_params=_cparams(2), name="retention")(p_ret.reshape(b, s, RET_SLAB), cos_t, sin_t, dmat, qw, kw, gch)


def _retention_consts(s):
    half = HEAD_DIM // 2
    inv = 1.0 / (ROPE_BASE ** (jnp.arange(0, HEAD_DIM, 2, dtype=F32) / HEAD_DIM))
    ang = jnp.arange(s, dtype=F32)[:, None] * inv[None, :]
    cos, sin = jnp.cos(ang), jnp.sin(ang)
    cos_t = jnp.tile(cos, (1, LANES // half))
    sin_t = jnp.tile(jnp.concatenate([-sin, sin], axis=-1), (1, LANES // HEAD_DIM))
    c = RET_CHUNK
    log_g = jnp.log1p(-jnp.exp2(-5.0 - jnp.arange(RET_H, dtype=F32)))
    pos = jnp.arange(c, dtype=F32)
    diff = pos[:, None] - pos[None, :]
    causal = diff >= 0
    dmat = jnp.where(causal[None], jnp.exp(jnp.where(causal, diff, 0.0)[None] * log_g[:, None, None]), 0.0)
    lane_g = jnp.repeat(log_g, HEAD_DIM)[None, :]
    qw = jnp.exp((pos + 1.0)[:, None] * lane_g)
    kw = jnp.exp((c - 1.0 - pos)[:, None] * lane_g)
    gch = jnp.exp(c * lane_g)
    return cos_t, sin_t, dmat, qw, kw, gch


def _rwkv_kernel(p_ref, mu_ref, wwa_ref, w0_ref, a0_ref, g2_ref, kk_ref, ka_ref, rk_ref, gng_ref, gnb_ref,
                 o_ref, st_ref, carry_ref, rm_s, g_s, y0_s):
    tb = p_ref.shape[0]
    c = RWKV_CHUNK
    w = RWKV_W
    n_ch = tb // c
    pairs = [slice(pi * LANES, (pi + 1) * LANES) for pi in range(N_PAIRS)]

    @pl.when(pl.program_id(1) == 0)
    def _():
        st_ref[...] = jnp.zeros(st_ref.shape, F32)
        carry_ref[...] = jnp.zeros(carry_ref.shape, F32)

    m_a, row, col, same_head = _pair_masks()
    m_b = jnp.logical_not(m_a)
    head_ones = jnp.where(same_head, 1.0, 0.0).astype(BF16)

    def head_sums(x):
        return jnp.concatenate([_split_dot(x[:, ln], head_ones, 2, True) for ln in pairs], axis=1)

    p = p_ref[...]
    rowi = lax.broadcasted_iota(I32, (tb, 1), 0)
    prev = jnp.where(rowi == 0, carry_ref[...], pltpu.roll(p, 1, 0))
    carry_ref[...] = p[tb - 1:tb, :]
    z = p + (prev - p) * mu_ref[...]
    r = z[:, 0:w]
    kr = z[:, w:2 * w]
    vr = z[:, 2 * w:3 * w]
    wa = z[:, 3 * w:3 * w + LANES]
    gl = z[:, 3 * w + LANES:]
    lane = lax.broadcasted_iota(I32, (1, LANES), 1)
    pre = _dot_bf(jnp.where(lane < RWKV_W_LORA, jnp.tanh(wa), wa), wwa_ref[...])
    w_log = -_softplus(-(w0_ref[...] + pre[:, :w])) - 0.5
    a = _sigmoid(a0_ref[...] + pre[:, w:])
    g_rw = _dot_bf(_sigmoid(gl), g2_ref[...])
    kk = kr * kk_ref[...]
    kk = kk / jnp.maximum(jnp.sqrt(head_sums(kk * kk)), 1e-12)
    kmod = kr * (1.0 + (a - 1.0) * ka_ref[...])
    bonus = head_sums(r * kmod * rk_ref[...]) * vr
    ld = -jnp.exp(w_log)

    t_r = lax.broadcasted_iota(I32, (tb, tb), 0)
    t_c = lax.broadcasted_iota(I32, (tb, tb), 1)
    same_chunk = (t_r >> RWKV_CHUNK_SHIFT) == (t_c >> RWKV_CHUNK_SHIFT)
    lc = _split_dot(jnp.where(same_chunk & (t_r >= t_c), 1.0, 0.0).astype(BF16), ld, 3, False)
    lc_end = _split_dot(jnp.where(same_chunk, 1.0, 0.0).astype(BF16), ld, 3, False)
    p_inv = jnp.exp(-lc)
    p_rem = jnp.exp(lc_end - lc)
    p_end = jnp.exp(lc_end)
    b0 = kk * a

    def batch(x):
        return jnp.stack([jnp.concatenate([jnp.where(m_a, x[ci * c:(ci + 1) * c, ln], 0.0),
                                           jnp.where(m_b, x[ci * c:(ci + 1) * c, ln], 0.0)], axis=0)
                          for ci in range(n_ch) for ln in pairs])

    lr = batch(r * jnp.exp(lc))
    la_b = batch(-kk * jnp.exp(lc - ld)).astype(BF16)
    lr_b = lr.astype(BF16)
    rb_b = batch(b0 * p_inv).astype(BF16)
    rk_b = batch(kmod * p_inv).astype(BF16)
    vs_b = batch(vr).astype(BF16)
    be_t = jnp.swapaxes(batch(b0 * p_rem), 1, 2)
    ke_t = jnp.swapaxes(batch(kmod * p_rem), 1, 2)
    pe = jnp.stack([p_end[ci * c:ci * c + 1, ln] for ci in range(n_ch) for ln in pairs])

    tri_incl = (row >= col)[None]
    tri_strict = (row > col)[None]
    eye = (row == col)[None]
    xx = jnp.einsum('nik,njk->nij', jnp.concatenate([la_b, lr_b], axis=1), jnp.concatenate([rb_b, rk_b], axis=1),
                    preferred_element_type=F32)
    x_ab = jnp.where(tri_strict, xx[:, :LANES, :LANES], 0.0)
    x_ak = jnp.where(tri_strict, xx[:, :LANES, LANES:], 0.0)
    x_rb = jnp.where(tri_incl, xx[:, LANES:, :LANES], 0.0)
    x_rk = jnp.where(tri_incl, xx[:, LANES:, LANES:], 0.0)
    inv = None
    for sh in range(RWKV_CHUNK_SHIFT):
        lv = (((row >> (sh + 1)) == (col >> (sh + 1))) & (((row >> sh) & 1) == 1) & (((col >> sh) & 1) == 0))[None]
        x_lv = jnp.where(lv, x_ab, 0.0)
        inv = jnp.where(eye, 1.0, 0.0) + x_lv if inv is None else inv + _bmm(_bmm(inv, x_lv), inv)
    wy = _bmm(jnp.concatenate([x_ak, x_rk], axis=1), vs_b)
    aw_b = _bmm(inv, jnp.concatenate([la_b, wy[:, :LANES].astype(BF16)], axis=2)).astype(BF16)
    ba = _bmm(be_t, aw_b)
    m_mat = jnp.where(eye, pe, 0.0) + ba[:, :, :LANES]
    ra = _bmm(x_rb, aw_b)
    rm_s[...] = jnp.concatenate([lr + ra[:, :, :LANES], m_mat], axis=1).astype(BF16)
    g_s[...] = ba[:, :, LANES:] + _bmm(ke_t, vs_b)
    y0_s[...] = ra[:, :, LANES:] + wy[:, LANES:]

    st = [st_ref[pi] for pi in range(N_PAIRS)]
    y_rows = []
    for ci in range(n_ch):
        ys = []
        for pi in range(N_PAIRS):
            n = ci * N_PAIRS + pi
            sy = jnp.dot(rm_s[n], st[pi].astype(BF16), preferred_element_type=F32)
            yst = sy[:LANES] + y0_s[n]
            st[pi] = sy[LANES:] + g_s[n]
            ys.append(yst[:c] + yst[c:])
        y_rows.append(jnp.concatenate(ys, axis=1))
    for pi in range(N_PAIRS):
        st_ref[pi] = st[pi]
    y = jnp.concatenate(y_rows, axis=0)

    yn = jnp.concatenate([_head_norm_pair(y[:, ln], same_head, RWKV_GN_EPS) for ln in pairs], axis=1)
    o_ref[...] = (yn * gng_ref[...] + gnb_ref[...] + bonus) * g_rw


def _rwkv(p_rwkv, params, b, s):
    tb = min(256, s)
    n = (tb // RWKV_CHUNK) * N_PAIRS
    full = lambda a: pl.BlockSpec(a.shape, lambda i, j: (0,) * a.ndim)
    return pl.pallas_call(
        _rwkv_kernel, grid=(b, s // tb),
        in_specs=[pl.BlockSpec((None, tb, RWKV_SLAB), lambda i, j: (i, j, 0))] + [full(a) for a in params],
        out_specs=pl.BlockSpec((None, tb, RWKV_W), lambda i, j: (i, j, 0)),
        out_shape=jax.ShapeDtypeStruct((b, s, RWKV_W), F32),
        scratch_shapes=[pltpu.VMEM((N_PAIRS, LANES, LANES), F32), pltpu.VMEM((1, RWKV_SLAB), F32),
                        pltpu.VMEM((n, 2 * LANES, LANES), BF16), pltpu.VMEM((n, LANES, LANES), F32),
                        pltpu.VMEM((n, LANES, LANES), F32)],
        compiler_params=_cparams(2), name="rwkv7")(p_rwkv.reshape(b, s, RWKV_SLAB), *params)


def _rwkv_params(mu, w0, w2, a0, a2, g2, k_k, k_a, r_k, gn_g, gn_b):
    row = lambda v: v.reshape(1, -1)
    z = jnp.zeros((RWKV_W_LORA, RWKV_W), F32)
    wwa = jnp.concatenate([jnp.concatenate([w2, z], axis=1), jnp.concatenate([z, a2], axis=1)], axis=0)
    return (row(mu), wwa.astype(BF16), row(w0), row(a0), g2.astype(BF16), row(k_k), row(k_a), row(r_k), row(gn_g), row(gn_b))


def _lru_kernel(p_ref, cw_ref, cb_ref, wax_ref, bax_ref, lam_ref, o_ref, xs_ref, h_ref):
    tb = p_ref.shape[0]
    w = LRU_W

    @pl.when(pl.program_id(1) == 0)
    def _():
        xs_ref[0:SUBLANES, :] = jnp.zeros((SUBLANES, w), F32)
        h_ref[...] = jnp.zeros(h_ref.shape, F32)

    x = p_ref[:, 0:w]
    gate = p_ref[:, w:]
    xs_ref[SUBLANES:, :] = x
    xl = x * cw_ref[LRU_CONV - 1:LRU_CONV, :] + cb_ref[...]
    for d in range(1, LRU_CONV):
        xl = xl + xs_ref[pl.ds(SUBLANES - d, tb), :] * cw_ref[LRU_CONV - 1 - d:LRU_CONV - d, :]
    xs_ref[0:SUBLANES, :] = x[tb - SUBLANES:, :]
    gates = _dot_bf(xl, wax_ref[...]) + bax_ref[...]
    r_gate = _sigmoid(gates[:, :w])
    i_gate = _sigmoid(gates[:, w:])
    log_a = -LRU_C * r_gate * _softplus(-lam_ref[...])
    a = jnp.exp(log_a)
    bv = jnp.sqrt((1.0 - a) * (1.0 + a)) * (i_gate * xl)
    rowi = lax.broadcasted_iota(I32, (tb, 1), 0)
    d = 1
    while d < tb:
        keep = rowi >= d
        a_sh = jnp.where(keep, pltpu.roll(a, d, 0), 1.0)
        b_sh = jnp.where(keep, pltpu.roll(bv, d, 0), 0.0)
        bv = a * b_sh + bv
        a = a * a_sh
        d *= 2
    h = bv + a * h_ref[...]
    h_ref[...] = h[tb - 1:tb, :]
    o_ref[...] = h * _gelu_tanh(gate)


def _lru(p_lru, params, b, s):
    tb = min(512, s)
    full = lambda a: pl.BlockSpec(a.shape, lambda i, j: (0,) * a.ndim)
    return pl.pallas_call(
        _lru_kernel, grid=(b, s // tb),
        in_specs=[pl.BlockSpec((None, tb, LRU_SLAB), lambda i, j: (i, j, 0))] + [full(a) for a in params],
        out_specs=pl.BlockSpec((None, tb, LRU_W), lambda i, j: (i, j, 0)),
        out_shape=jax.ShapeDtypeStruct((b, s, LRU_W), F32),
        scratch_shapes=[pltpu.VMEM((tb + SUBLANES, LRU_W), F32), pltpu.VMEM((1, LRU_W), F32)],
        compiler_params=_cparams(2), name="rglru")(p_lru.reshape(b, s, LRU_SLAB), *params)


def _lru_params(conv_w, conv_b, wa, ba, wx, bx, lam):
    row = lambda v: v.reshape(1, -1)
    def bd(m):
        out = jnp.zeros((LRU_W, LRU_W), F32)
        for i in range(LRU_H):
            out = out.at[i * HEAD_DIM:(i + 1) * HEAD_DIM, i * HEAD_DIM:(i + 1) * HEAD_DIM].set(m[i])
        return out
    wax = jnp.concatenate([bd(wa), bd(wx)], axis=1)
    return (conv_w, row(conv_b), wax.astype(BF16), jnp.concatenate([row(ba), row(bx)], axis=1), row(lam))


def _outproj_kernel(yr_ref, yw_ref, yl_ref, h_ref, w_ref, g_ref, b_ref, o_ref):
    ycat = jnp.concatenate([yr_ref[...], yw_ref[...], yl_ref[...]], axis=-1).astype(BF16)
    mix = jnp.dot(ycat, w_ref[...], preferred_element_type=F32)
    o_ref[...] = _layer_norm(ALPHA * h_ref[...] + mix, g_ref[...], b_ref[...])


def _out_projection(y_ret, y_rwkv, y_lru, h, w_bf, g, b):
    t = h.shape[0]
    tm = min(512, t)
    row = lambda w: pl.BlockSpec((tm, w), lambda i: (i, 0))
    full = lambda a: pl.BlockSpec(a.shape, lambda i: (0,) * a.ndim)
    return pl.pallas_call(
        _outproj_kernel, grid=(t // tm,),
        in_specs=[row(RET_W), row(RWKV_W), row(LRU_W), row(D_MODEL), full(w_bf), full(g), full(b)],
        out_specs=row(D_MODEL), out_shape=jax.ShapeDtypeStruct((t, D_MODEL), F32),
        compiler_params=_cparams(1), name="out_proj_ln")(y_ret, y_rwkv, y_lru, h, w_bf, g, b)


def _router_kernel(h_ref, whi_ref, wlo_ref, b_ref, ew_ref, ei_ref, cnt_ref):
    @pl.when(pl.program_id(0) == 0)
    def _():
        cnt_ref[...] = jnp.zeros(cnt_ref.shape, F32)

    h = h_ref[...]
    h_hi = h.astype(BF16)
    h_lo = (h - h_hi.astype(F32)).astype(BF16)
    logits = (jnp.dot(h_hi, whi_ref[...], preferred_element_type=F32) + jnp.dot(h_lo, whi_ref[...], preferred_element_type=F32)
              + jnp.dot(h_hi, wlo_ref[...], preferred_element_type=F32)) + b_ref[...]
    lane = lax.broadcasted_iota(I32, logits.shape, 1)
    neg = -jnp.inf
    gmask = lane < N_GROUPS
    gl = jnp.where(gmask, logits, neg)
    gmax = jnp.max(gl, -1, keepdims=True)
    g_sel = jnp.min(jnp.where(gl == gmax, lane, LANES), -1, keepdims=True)
    g_gate = 1.0 / jnp.sum(jnp.where(gmask, jnp.exp(gl - gmax), 0.0), -1, keepdims=True)
    emask = (lane >= N_GROUPS) & (lane < N_GROUPS + N_EXPERTS) & (((lane - N_GROUPS) >> GROUP_SHIFT) == g_sel)
    el = jnp.where(emask, logits, neg)
    emax = jnp.max(el, -1, keepdims=True)
    ee = jnp.where(emask, jnp.exp(el - emax), 0.0)
    pe = ee / jnp.sum(ee, -1, keepdims=True)
    pe = jnp.where(emask, pe, -1.0)
    p1 = jnp.max(pe, -1, keepdims=True)
    i1 = jnp.min(jnp.where(pe == p1, lane, LANES), -1, keepdims=True)
    pe2 = jnp.where(lane == i1, -1.0, pe)
    p2 = jnp.max(pe2, -1, keepdims=True)
    i2 = jnp.min(jnp.where(pe2 == p2, lane, LANES), -1, keepdims=True)
    tot = p1 + p2
    ew_ref[...] = jnp.where(lane == 0, g_gate * (p1 / tot), jnp.where(lane == 1, g_gate * (p2 / tot), 0.0))
    tm = logits.shape[0]
    chosen = jnp.where((lane == i1) | (lane == i2), 1.0, 0.0)
    before = (lax.broadcasted_iota(I32, (tm, tm), 0) > lax.broadcasted_iota(I32, (tm, tm), 1))
    cum = jnp.dot(jnp.where(before, 1.0, 0.0).astype(BF16), chosen.astype(BF16), preferred_element_type=F32) + cnt_ref[0:1, :]
    r1 = jnp.sum(jnp.where(lane == i1, cum, 0.0), -1, keepdims=True).astype(I32)
    r2 = jnp.sum(jnp.where(lane == i2, cum, 0.0), -1, keepdims=True).astype(I32)
    cnt_ref[...] = jnp.broadcast_to(cnt_ref[0:1, :] + jnp.sum(chosen, 0, keepdims=True), cnt_ref.shape)
    ei_ref[...] = jnp.where(lane == 0, i1 - N_GROUPS, jnp.where(lane == 1, i2 - N_GROUPS,
                            jnp.where(lane == 2, r1, jnp.where(lane == 3, r2, 0))))


def _router(h, w_r, b_r):
    t = h.shape[0]
    tm = min(512, t)
    w_hi = w_r.astype(BF16)
    w_lo = (w_r - w_hi.astype(F32)).astype(BF16)
    row = lambda w: pl.BlockSpec((tm, w), lambda i: (i, 0))
    full = lambda a: pl.BlockSpec(a.shape, lambda i: (0,) * a.ndim)
    return pl.pallas_call(
        _router_kernel, grid=(t // tm,), in_specs=[row(D_MODEL), full(w_hi), full(w_lo), full(b_r)],
        out_specs=[row(LANES), row(LANES), pl.BlockSpec((SUBLANES, LANES), lambda i: (0, 0))],
        out_shape=[jax.ShapeDtypeStruct((t, LANES), F32), jax.ShapeDtypeStruct((t, LANES), I32),
                   jax.ShapeDtypeStruct((SUBLANES, LANES), F32)],
        compiler_params=_cparams(1), name="router")(h, w_hi, w_lo, b_r)


DMA_UNROLL = 8


def _dispatch_kernel(dest_ref, h_ref, xs_in_ref, xs_ref, sem):
    del xs_in_ref
    td = h_ref.shape[0]
    base = pl.program_id(0) * td

    def copy(r, j):
        return pltpu.make_async_copy(h_ref.at[pl.ds(r, 1)], xs_ref.at[pl.ds(dest_ref[2 * (base + r) + j], 1)], sem)

    def start(r, c):
        copy(r, 0).start(priority=0)
        copy(r, 1).start(priority=1)
        return c

    def wait(r, c):
        copy(r, 0).wait()
        copy(r, 1).wait()
        return c

    lax.fori_loop(0, td, start, 0, unroll=DMA_UNROLL)
    lax.fori_loop(0, td, wait, 0, unroll=DMA_UNROLL)


def _dispatch(h, dest, n_slots):
    t = h.shape[0]
    td = min(1024, t)
    xs0 = jnp.zeros((n_slots, D_MODEL), F32)
    any_spec = pl.BlockSpec(memory_space=pl.ANY)
    return pl.pallas_call(
        _dispatch_kernel,
        grid_spec=pltpu.PrefetchScalarGridSpec(
            num_scalar_prefetch=1, grid=(t // td,),
            in_specs=[pl.BlockSpec((td, D_MODEL), lambda i, d: (i, 0)), any_spec], out_specs=any_spec,
            scratch_shapes=[pltpu.SemaphoreType.DMA(())]),
        out_shape=jax.ShapeDtypeStruct((n_slots, D_MODEL), F32),
        input_output_aliases={2: 0},
        compiler_params=_cparams(1), name="moe_dispatch")(dest, h, xs0)


def _expert_kernel(be_ref, nu_ref, x_ref, wg_ref, wu_ref, wd_ref, o_ref, wg_s, wu_s, wd_s):
    i = pl.program_id(0)
    used = i < nu_ref[0]
    new_expert = (i == 0) | (be_ref[i] != be_ref[jnp.maximum(i - 1, 0)])

    @pl.when(used & new_expert)
    def _():
        wg_s[...] = wg_ref[...].astype(BF16)
        wu_s[...] = wu_ref[...].astype(BF16)
        wd_s[...] = wd_ref[...].astype(BF16)

    @pl.when(used)
    def _():
        x = x_ref[...].astype(BF16)
        hid = _silu(jnp.dot(x, wg_s[...], preferred_element_type=F32)) * jnp.dot(x, wu_s[...], preferred_element_type=F32)
        o_ref[...] = jnp.dot(hid.astype(BF16), wd_s[...], preferred_element_type=F32)

    @pl.when(jnp.logical_not(used))
    def _():
        o_ref[...] = jnp.zeros(o_ref.shape, F32)


def _experts(xs, block_exp, n_used, wg, wu, wd, layer):
    n_blocks = xs.shape[0] // MOE_BLOCK
    w_spec = lambda r, c: pl.BlockSpec((None, None, r, c), lambda i, be, nu: (layer, be[i], 0, 0))
    return pl.pallas_call(
        _expert_kernel,
        grid_spec=pltpu.PrefetchScalarGridSpec(
            num_scalar_prefetch=2, grid=(n_blocks,),
            in_specs=[pl.BlockSpec((MOE_BLOCK, D_MODEL), lambda i, be, nu: (i, 0)),
                      w_spec(D_MODEL, D_EXPERT), w_spec(D_MODEL, D_EXPERT), w_spec(D_EXPERT, D_MODEL)],
            out_specs=pl.BlockSpec((MOE_BLOCK, D_MODEL), lambda i, be, nu: (i, 0)),
            scratch_shapes=[pltpu.VMEM((D_MODEL, D_EXPERT), BF16), pltpu.VMEM((D_MODEL, D_EXPERT), BF16),
                            pltpu.VMEM((D_EXPERT, D_MODEL), BF16)]),
        out_shape=jax.ShapeDtypeStruct(xs.shape, F32),
        compiler_params=_cparams(1), name="moe_experts")(block_exp, n_used, xs, wg, wu, wd)


def _combine_kernel(dest_ref, yb_ref, ew_ref, h_ref, g_ref, b_ref, o_ref, buf, sems):
    tc = h_ref.shape[0]
    i = pl.program_id(0)
    n = pl.num_programs(0)

    def copies(step, slot, fn):
        def body(r, c):
            tok = step * tc + r
            for j in range(2):
                fn(pltpu.make_async_copy(yb_ref.at[pl.ds(dest_ref[2 * tok + j], 1)],
                                         buf.at[slot, j, pl.ds(r, 1)], sems.at[slot]), j)
            return c
        lax.fori_loop(0, tc, body, 0, unroll=DMA_UNROLL)

    start = lambda cp, j: cp.start(priority=j)

    @pl.when(i == 0)
    def _():
        copies(0, 0, start)

    @pl.when(i + 1 < n)
    def _():
        copies(i + 1, (i + 1) & 1, start)

    slot = i & 1
    copies(i, slot, lambda cp, j: cp.wait())
    ew = ew_ref[...]
    y = ew[:, 0:1] * buf[slot, 0] + ew[:, 1:2] * buf[slot, 1]
    o_ref[...] = _layer_norm(ALPHA * h_ref[...] + y, g_ref[...], b_ref[...])


def _combine(yb, dest, ew, h, g, b):
    t = h.shape[0]
    tc = min(256, t)
    row = lambda w: pl.BlockSpec((tc, w), lambda i, d: (i, 0))
    full = lambda a: pl.BlockSpec(a.shape, lambda i, d: (0,) * a.ndim)
    return pl.pallas_call(
        _combine_kernel,
        grid_spec=pltpu.PrefetchScalarGridSpec(
            num_scalar_prefetch=1, grid=(t // tc,),
            in_specs=[pl.BlockSpec(memory_space=pl.ANY), row(LANES), row(D_MODEL), full(g), full(b)],
            out_specs=row(D_MODEL),
            scratch_shapes=[pltpu.VMEM((2, 2, tc, D_MODEL), F32), pltpu.SemaphoreType.DMA((2,))]),
        out_shape=jax.ShapeDtypeStruct((t, D_MODEL), F32),
        compiler_params=_cparams(1), name="moe_combine_ln")(dest, yb, ew, h, g, b)


def _moe_plan(ei, cnt, t):
    a = t * 2
    counts = cnt[0, N_GROUPS:N_GROUPS + N_EXPERTS].astype(I32)
    padded = ((counts + MOE_BLOCK - 1) // MOE_BLOCK) * MOE_BLOCK
    pends = jnp.cumsum(padded)
    pstarts = pends - padded
    e_ids = jnp.arange(N_EXPERTS, dtype=I32)[None, None, :]
    start_of = jnp.sum(jnp.where(ei[:, 0:2, None] == e_ids, pstarts[None, None, :], 0), axis=-1)
    dest = (start_of + ei[:, 2:4]).reshape(a).astype(I32)
    n_blocks = -(-a // MOE_BLOCK) + N_EXPERTS
    block_exp = jnp.minimum(jnp.searchsorted(pends, jnp.arange(n_blocks) * MOE_BLOCK, side='right'), N_EXPERTS - 1).astype(I32)
    n_used = (pends[-1:] // MOE_BLOCK).astype(I32)
    return dest, block_exp, n_used, n_blocks * MOE_BLOCK


def _hier_moe_ln(h, w_r, b_r, wg, wu, wd, layer, g, b):
    t = h.shape[0]
    ew, ei, cnt = _router(h, w_r, b_r)
    dest, block_exp, n_used, n_slots = _moe_plan(ei, cnt, t)
    xs = _dispatch(h, dest, n_slots)
    yb = _experts(xs, block_exp, n_used, wg, wu, wd, layer)
    return _combine(yb, dest, ew, h, g, b)


def kernel(x, ln_in_g, ln_in_b, w_in, w_out, rwkv_mu, rwkv_w0, rwkv_w2, rwkv_a0, rwkv_a2, rwkv_g2, rwkv_k_k, rwkv_k_a, rwkv_r_k, rwkv_gn_g, rwkv_gn_b, lru_conv_w, lru_conv_b, lru_wa, lru_ba, lru_wx, lru_bx, lru_lambda, ln1_g, ln1_b, moe_wg, moe_bg, moe_we, moe_be, moe_w_gate, moe_w_up, moe_w_down, ln2_g, ln2_b):
    b, s, d = x.shape
    t = b * s
    row = lambda v: v.reshape(1, -1)
    ret_consts = _retention_consts(s)
    h = x.reshape(t, d)
    for l in range(DEPTH):
        w_in_bf = w_in[l].astype(BF16)
        if l == 0:
            h, p_ret, p_rwkv, p_lru = _in_projection(h, w_in_bf, ln=(row(ln_in_g), row(ln_in_b)))
        else:
            p_ret, p_rwkv, p_lru = _in_projection(h, w_in_bf)
        y_ret = _retention(p_ret, ret_consts, b, s)
        y_rwkv = _rwkv(p_rwkv, _rwkv_params(rwkv_mu[l], rwkv_w0[l], rwkv_w2[l], rwkv_a0[l], rwkv_a2[l], rwkv_g2[l],
                                            rwkv_k_k[l], rwkv_k_a[l], rwkv_r_k[l], rwkv_gn_g[l], rwkv_gn_b[l]), b, s)
        y_lru = _lru(p_lru, _lru_params(lru_conv_w[l], lru_conv_b[l], lru_wa[l], lru_ba[l], lru_wx[l], lru_bx[l],
                                        lru_lambda[l]), b, s)
        h = _out_projection(y_ret.reshape(t, RET_W), y_rwkv.reshape(t, RWKV_W), y_lru.reshape(t, LRU_W), h,
                            w_out[l].astype(BF16), row(ln1_g[l]), row(ln1_b[l]))
        pad = jnp.zeros((d, LANES - N_GROUPS - N_EXPERTS), F32)
        w_r = jnp.concatenate([moe_wg[l], moe_we[l], pad], axis=1)
        b_r = jnp.concatenate([moe_bg[l], moe_be[l], jnp.zeros((LANES - N_GROUPS - N_EXPERTS,), F32)]).reshape(1, LANES)
        h = _hier_moe_ln(h, w_r, b_r, moe_w_gate, moe_w_up, moe_w_down, l, row(ln2_g[l]), row(ln2_b[l]))
    return h.reshape(b, s, d)
```

```python
import functools

import jax
import jax.numpy as jnp
from jax import lax
from jax.experimental import pallas as pl
from jax.experimental.pallas import tpu as pltpu

F32 = jnp.float32
BF16 = jnp.bfloat16
I32 = jnp.int32

D_MODEL = 1024
DEPTH = 2
HEAD_DIM = 64
HEAD_SHIFT = 6
RET_H = 6
RET_W = RET_H * HEAD_DIM
RET_CHUNK = 128
ROPE_BASE = 10000.0
RWKV_H = 6
RWKV_W = RWKV_H * HEAD_DIM
RWKV_W_LORA = 64
RWKV_A_LORA = 64
RWKV_G_LORA = 128
RWKV_GN_EPS = 64e-5
LRU_H = 4
LRU_W = LRU_H * HEAD_DIM
LRU_CONV = 4
LRU_C = 8.0
RET_SLAB = 4 * RET_W
RWKV_SLAB = 3 * RWKV_W + RWKV_W_LORA + RWKV_A_LORA + RWKV_G_LORA
LRU_SLAB = 2 * LRU_W
IN_W = RET_SLAB + RWKV_SLAB + LRU_SLAB
N_GROUPS = 4
EXPERTS_PER_GROUP = 8
GROUP_SHIFT = 3
N_EXPERTS = N_GROUPS * EXPERTS_PER_GROUP
D_EXPERT = D_MODEL // 2
MOE_BLOCK = 256
LN_EPS = 1e-5
ALPHA = (2.0 * DEPTH) ** 0.25

LANES = 128
SUBLANES = 8
N_PAIRS = RET_W // LANES
RWKV_CHUNK = 64
RWKV_CHUNK_SHIFT = 6
VMEM_LIMIT = 56 * 1024 * 1024


def _cparams(n_axes, vmem=VMEM_LIMIT):
    return pltpu.CompilerParams(dimension_semantics=("arbitrary",) * n_axes, vmem_limit_bytes=vmem)


def _dot_bf(a, b):
    return jnp.dot(a.astype(BF16), b.astype(BF16), preferred_element_type=F32)


def _split_dot(a, b, passes, split_lhs):
    rem = a if split_lhs else b
    acc = None
    for _ in range(passes):
        piece = rem.astype(BF16)
        part = jnp.dot(piece, b, preferred_element_type=F32) if split_lhs else jnp.dot(a, piece, preferred_element_type=F32)
        acc = part if acc is None else acc + part
        rem = rem - piece.astype(F32)
    return acc


def _bmm(a, b):
    return jnp.einsum('nij,njk->nik', a.astype(BF16), b.astype(BF16), preferred_element_type=F32)


def _sigmoid(x):
    return 1.0 / (1.0 + jnp.exp(-x))


def _silu(x):
    return x * _sigmoid(x)


def _softplus(x):
    return jnp.maximum(x, 0.0) + jnp.log1p(jnp.exp(-jnp.abs(x)))


def _gelu_tanh(x):
    return 0.5 * x * (1.0 + jnp.tanh(0.7978845608028654 * (x + 0.044715 * (x * x * x))))


def _layer_norm(x, g, b):
    mu = jnp.mean(x, -1, keepdims=True)
    xc = x - mu
    var = jnp.mean(xc * xc, -1, keepdims=True)
    return xc * lax.rsqrt(var + LN_EPS) * g + b


def _pair_masks():
    lane = lax.broadcasted_iota(I32, (1, LANES), 1)
    m_a = lane < HEAD_DIM
    row = lax.broadcasted_iota(I32, (LANES, LANES), 0)
    col = lax.broadcasted_iota(I32, (LANES, LANES), 1)
    same_head = (row >> HEAD_SHIFT) == (col >> HEAD_SHIFT)
    return m_a, row, col, same_head


def _head_norm_pair(y, same_head, eps):
    avg = jnp.where(same_head, 1.0 / HEAD_DIM, 0.0).astype(BF16)
    mu = _split_dot(y, avg, 2, True)
    yc = y - mu
    var = _split_dot(yc * yc, avg, 2, True)
    return yc * lax.rsqrt(var + eps)


def _inproj_ln_kernel(x_ref, g_ref, b_ref, w_ref, h_ref, pr_ref, pw_ref, pl_ref):
    h = _layer_norm(x_ref[...], g_ref[...], b_ref[...])
    h_ref[...] = h
    p = jnp.dot(h.astype(BF16), w_ref[...], preferred_element_type=F32)
    pr_ref[...] = p[:, :RET_SLAB]
    pw_ref[...] = p[:, RET_SLAB:RET_SLAB + RWKV_SLAB]
    pl_ref[...] = p[:, RET_SLAB + RWKV_SLAB:]


def _inproj_kernel(h_ref, w_ref, pr_ref, pw_ref, pl_ref):
    p = jnp.dot(h_ref[...].astype(BF16), w_ref[...], preferred_element_type=F32)
    pr_ref[...] = p[:, :RET_SLAB]
    pw_ref[...] = p[:, RET_SLAB:RET_SLAB + RWKV_SLAB]
    pl_ref[...] = p[:, RET_SLAB + RWKV_SLAB:]


def _in_projection(x2d, w_bf, ln=None):
    t = x2d.shape[0]
    tm = min(512, t)
    row = lambda w: pl.BlockSpec((tm, w), lambda i: (i, 0))
    full = lambda a: pl.BlockSpec(a.shape, lambda i: (0,) * a.ndim)
    outs = [jax.ShapeDtypeStruct((t, RET_SLAB), F32), jax.ShapeDtypeStruct((t, RWKV_SLAB), F32),
            jax.ShapeDtypeStruct((t, LRU_SLAB), F32)]
    out_specs = [row(RET_SLAB), row(RWKV_SLAB), row(LRU_SLAB)]
    if ln is None:
        return pl.pallas_call(
            _inproj_kernel, grid=(t // tm,), in_specs=[row(D_MODEL), full(w_bf)], out_specs=out_specs,
            out_shape=outs, compiler_params=_cparams(1), name="in_proj")(x2d, w_bf)
    g, b = ln
    return pl.pallas_call(
        _inproj_ln_kernel, grid=(t // tm,), in_specs=[row(D_MODEL), full(g), full(b), full(w_bf)],
        out_specs=[row(D_MODEL)] + out_specs, out_shape=[jax.ShapeDtypeStruct((t, D_MODEL), F32)] + outs,
        compiler_params=_cparams(1), name="ln_in_proj")(x2d, g, b, w_bf)


def _swap_halves(t):
    n = t.shape[-1]
    lane = lax.broadcasted_iota(I32, (1, n), 1)
    lower = (lane & (HEAD_DIM - 1)) < (HEAD_DIM // 2)
    return jnp.where(lower, pltpu.roll(t, n - HEAD_DIM // 2, 1), pltpu.roll(t, HEAD_DIM // 2, 1))


def _retention_kernel(p_ref, cos_ref, sin_ref, dmat_ref, qw_ref, kw_ref, gch_ref, o_ref, st_ref):
    @pl.when(pl.program_id(1) == 0)
    def _():
        st_ref[...] = jnp.zeros(st_ref.shape, F32)

    tb = p_ref.shape[0]
    c = RET_CHUNK
    n_ch = tb // c
    pairs = [slice(pi * LANES, (pi + 1) * LANES) for pi in range(N_PAIRS)]
    m_a, _, _, same_head = _pair_masks()
    m_b = jnp.logical_not(m_a)
    cos = jnp.concatenate([cos_ref[...]] * N_PAIRS, axis=-1)
    sin = jnp.concatenate([sin_ref[...]] * N_PAIRS, axis=-1)
    q = p_ref[:, 0:RET_W]
    k = p_ref[:, RET_W:2 * RET_W]
    v = p_ref[:, 2 * RET_W:3 * RET_W]
    q = q * cos + _swap_halves(q) * sin
    k = (k * cos + _swap_halves(k) * sin) * (HEAD_DIM ** -0.5)

    def per_problem(fn):
        return jnp.stack([fn(slice(ci * c, (ci + 1) * c), pi) for ci in range(n_ch) for pi in range(N_PAIRS)])

    def stack(x):
        return jnp.concatenate([jnp.where(m_a, x, 0.0), jnp.where(m_b, x, 0.0)], axis=0)

    k_b = per_problem(lambda rows, pi: k[rows, pairs[pi]]).astype(BF16)
    v_b = per_problem(lambda rows, pi: v[rows, pairs[pi]]).astype(BF16)
    q_st = per_problem(lambda rows, pi: stack(q[rows, pairs[pi]])).astype(BF16)
    v_st = per_problem(lambda rows, pi: stack(v[rows, pairs[pi]])).astype(BF16)
    q_w = per_problem(lambda rows, pi: q[rows, pairs[pi]] * qw_ref[:, pairs[pi]]).astype(BF16)
    kw_t = jnp.swapaxes(per_problem(lambda rows, pi: k[rows, pairs[pi]] * kw_ref[:, pairs[pi]]), 1, 2).astype(BF16)
    decay = per_problem(lambda rows, pi: jnp.concatenate([dmat_ref[2 * pi], dmat_ref[2 * pi + 1]], axis=0))

    s = jnp.einsum('nik,njk->nij', q_st, k_b, preferred_element_type=F32) * decay
    intra = _bmm(jnp.concatenate([s[:, :c], s[:, c:]], axis=2), v_st)
    kv = _bmm(kw_t, v_b)
    states = []
    for ci in range(n_ch):
        for pi in range(N_PAIRS):
            st = st_ref[pi]
            states.append(st)
            st_ref[pi] = st * gch_ref[:, pairs[pi]] + jnp.where(same_head, kv[ci * N_PAIRS + pi], 0.0)
    y_all = intra + _bmm(q_w, jnp.stack(states))
    y = jnp.concatenate([jnp.concatenate([y_all[ci * N_PAIRS + pi] for pi in range(N_PAIRS)], axis=1)
                         for ci in range(n_ch)], axis=0)
    yn = jnp.concatenate([_head_norm_pair(y[:, ln], same_head, LN_EPS) for ln in pairs], axis=1)
    o_ref[...] = yn * _silu(p_ref[:, 3 * RET_W:4 * RET_W])


def _retention(p_ret, consts, b, s):
    tb = min(512, s)
    cos_t, sin_t, dmat, qw, kw, gch = consts
    full = lambda a: pl.BlockSpec(a.shape, lambda i, j: (0,) * a.ndim)
    return pl.pallas_call(
        _retention_kernel, grid=(b, s // tb),
        in_specs=[pl.BlockSpec((None, tb, RET_SLAB), lambda i, j: (i, j, 0)),
                  pl.BlockSpec((tb, LANES), lambda i, j: (j, 0)), pl.BlockSpec((tb, LANES), lambda i, j: (j, 0)),
                  full(dmat), full(qw), full(kw), full(gch)],
        out_specs=pl.BlockSpec((None, tb, RET_W), lambda i, j: (i, j, 0)),
        out_shape=jax.ShapeDtypeStruct((b, s, RET_W), F32),
        scratch_shapes=[pltpu.VMEM((N_PAIRS, LANES, LANES), F32)],
        compiler_params=_cparams(2), name="retention")(p_ret.reshape(b, s, RET_SLAB), cos_t, sin_t, dmat, qw, kw, gch)


def _retention_consts(s):
    half = HEAD_DIM // 2
    inv = 1.0 / (ROPE_BASE ** (jnp.arange(0, HEAD_DIM, 2, dtype=F32) / HEAD_DIM))
    ang = jnp.arange(s, dtype=F32)[:, None] * inv[None, :]
    cos, sin = jnp.cos(ang), jnp.sin(ang)
    cos_t = jnp.tile(cos, (1, LANES // half))
    sin_t = jnp.tile(jnp.concatenate([-sin, sin], axis=-1), (1, LANES // HEAD_DIM))
    c = RET_CHUNK
    log_g = jnp.log1p(-jnp.exp2(-5.0 - jnp.arange(RET_H, dtype=F32)))
    pos = jnp.arange(c, dtype=F32)
    diff = pos[:, None] - pos[None, :]
    causal = diff >= 0
    dmat = jnp.where(causal[None], jnp.exp(jnp.where(causal, diff, 0.0)[None] * log_g[:, None, None]), 0.0)
    lane_g = jnp.repeat(log_g, HEAD_DIM)[None, :]
    qw = jnp.exp((pos + 1.0)[:, None] * lane_g)
    kw = jnp.exp((c - 1.0 - pos)[:, None] * lane_g)
    gch = jnp.exp(c * lane_g)
    return cos_t, sin_t, dmat, qw, kw, gch


def _rwkv_kernel(p_ref, mu_ref, wwa_ref, w0_ref, a0_ref, g2_ref, kk_ref, ka_ref, rk_ref, gng_ref, gnb_ref,
                 o_ref, st_ref, carry_ref, rm_s, g_s, y0_s):
    tb = p_ref.shape[0]
    c = RWKV_CHUNK
    w = RWKV_W
    n_ch = tb // c
    pairs = [slice(pi * LANES, (pi + 1) * LANES) for pi in range(N_PAIRS)]

    @pl.when(pl.program_id(1) == 0)
    def _():
        st_ref[...] = jnp.zeros(st_ref.shape, F32)
        carry_ref[...] = jnp.zeros(carry_ref.shape, F32)

    m_a, row, col, same_head = _pair_masks()
    m_b = jnp.logical_not(m_a)
    head_ones = jnp.where(same_head, 1.0, 0.0).astype(BF16)

    def head_sums(x):
        return jnp.concatenate([_split_dot(x[:, ln], head_ones, 2, True) for ln in pairs], axis=1)

    p = p_ref[...]
    rowi = lax.broadcasted_iota(I32, (tb, 1), 0)
    prev = jnp.where(rowi == 0, carry_ref[...], pltpu.roll(p, 1, 0))
    carry_ref[...] = p[tb - 1:tb, :]
    z = p + (prev - p) * mu_ref[...]
    r = z[:, 0:w]
    kr = z[:, w:2 * w]
    vr = z[:, 2 * w:3 * w]
    wa = z[:, 3 * w:3 * w + LANES]
    gl = z[:, 3 * w + LANES:]
    lane = lax.broadcasted_iota(I32, (1, LANES), 1)
    pre = _dot_bf(jnp.where(lane < RWKV_W_LORA, jnp.tanh(wa), wa), wwa_ref[...])
    w_log = -_softplus(-(w0_ref[...] + pre[:, :w])) - 0.5
    a = _sigmoid(a0_ref[...] + pre[:, w:])
    g_rw = _dot_bf(_sigmoid(gl), g2_ref[...])
    kk = kr * kk_ref[...]
    kk = kk / jnp.maximum(jnp.sqrt(head_sums(kk * kk)), 1e-12)
    kmod = kr * (1.0 + (a - 1.0) * ka_ref[...])
    bonus = head_sums(r * kmod * rk_ref[...]) * vr
    ld = -jnp.exp(w_log)

    t_r = lax.broadcasted_iota(I32, (tb, tb), 0)
    t_c = lax.broadcasted_iota(I32, (tb, tb), 1)
    same_chunk = (t_r >> RWKV_CHUNK_SHIFT) == (t_c >> RWKV_CHUNK_SHIFT)
    lc = _split_dot(jnp.where(same_chunk & (t_r >= t_c), 1.0, 0.0).astype(BF16), ld, 3, False)
    lc_end = _split_dot(jnp.where(same_chunk, 1.0, 0.0).astype(BF16), ld, 3, False)
    p_inv = jnp.exp(-lc)
    p_rem = jnp.exp(lc_end - lc)
    p_end = jnp.exp(lc_end)
    b0 = kk * a

    def batch(x):
        return jnp.stack([jnp.concatenate([jnp.where(m_a, x[ci * c:(ci + 1) * c, ln], 0.0),
                                           jnp.where(m_b, x[ci * c:(ci + 1) * c, ln], 0.0)], axis=0)
                          for ci in range(n_ch) for ln in pairs])

    lr = batch(r * jnp.exp(lc))
    la_b = batch(-kk * jnp.exp(lc - ld)).astype(BF16)
    lr_b = lr.astype(BF16)
    rb_b = batch(b0 * p_inv).astype(BF16)
    rk_b = batch(kmod * p_inv).astype(BF16)
    vs_b = batch(vr).astype(BF16)
    be_t = jnp.swapaxes(batch(b0 * p_rem), 1, 2)
    ke_t = jnp.swapaxes(batch(kmod * p_rem), 1, 2)
    pe = jnp.stack([p_end[ci * c:ci * c + 1, ln] for ci in range(n_ch) for ln in pairs])

    tri_incl = (row >= col)[None]
    tri_strict = (row > col)[None]
    eye = (row == col)[None]
    xx = jnp.einsum('nik,njk->nij', jnp.concatenate([la_b, lr_b], axis=1), jnp.concatenate([rb_b, rk_b], axis=1),
                    preferred_element_type=F32)
    x_ab = jnp.where(tri_strict, xx[:, :LANES, :LANES], 0.0)
    x_ak = jnp.where(tri_strict, xx[:, :LANES, LANES:], 0.0)
    x_rb = jnp.where(tri_incl, xx[:, LANES:, :LANES], 0.0)
    x_rk = jnp.where(tri_incl, xx[:, LANES:, LANES:], 0.0)
    inv = None
    for sh in range(RWKV_CHUNK_SHIFT):
        lv = (((row >> (sh + 1)) == (col >> (sh + 1))) & (((row >> sh) & 1) == 1) & (((col >> sh) & 1) == 0))[None]
        x_lv = jnp.where(lv, x_ab, 0.0)
        inv = jnp.where(eye, 1.0, 0.0) + x_lv if inv is None else inv + _bmm(_bmm(inv, x_lv), inv)
    wy = _bmm(jnp.concatenate([x_ak, x_rk], axis=1), vs_b)
    aw_b = _bmm(inv, jnp.concatenate([la_b, wy[:, :LANES].astype(BF16)], axis=2)).astype(BF16)
    ba = _bmm(be_t, aw_b)
    m_mat = jnp.where(eye, pe, 0.0) + ba[:, :, :LANES]
    ra = _bmm(x_rb, aw_b)
    rm_s[...] = jnp.concatenate([lr + ra[:, :, :LANES], m_mat], axis=1).astype(BF16)
    g_s[...] = ba[:, :, LANES:] + _bmm(ke_t, vs_b)
    y0_s[...] = ra[:, :, LANES:] + wy[:, LANES:]

    st = [st_ref[pi] for pi in range(N_PAIRS)]
    y_rows = []
    for ci in range(n_ch):
        ys = []
        for pi in range(N_PAIRS):
            n = ci * N_PAIRS + pi
            sy = jnp.dot(rm_s[n], st[pi].astype(BF16), preferred_element_type=F32)
            yst = sy[:LANES] + y0_s[n]
            st[pi] = sy[LANES:] + g_s[n]
            ys.append(yst[:c] + yst[c:])
        y_rows.append(jnp.concatenate(ys, axis=1))
    for pi in range(N_PAIRS):
        st_ref[pi] = st[pi]
    y = jnp.concatenate(y_rows, axis=0)

    yn = jnp.concatenate([_head_norm_pair(y[:, ln], same_head, RWKV_GN_EPS) for ln in pairs], axis=1)
    o_ref[...] = (yn * gng_ref[...] + gnb_ref[...] + bonus) * g_rw


def _rwkv(p_rwkv, params, b, s):
    tb = min(256, s)
    n = (tb // RWKV_CHUNK) * N_PAIRS
    full = lambda a: pl.BlockSpec(a.shape, lambda i, j: (0,) * a.ndim)
    return pl.pallas_call(
        _rwkv_kernel, grid=(b, s // tb),
        in_specs=[pl.BlockSpec((None, tb, RWKV_SLAB), lambda i, j: (i, j, 0))] + [full(a) for a in params],
        out_specs=pl.BlockSpec((None, tb, RWKV_W), lambda i, j: (i, j, 0)),
        out_shape=jax.ShapeDtypeStruct((b, s, RWKV_W), F32),
        scratch_shapes=[pltpu.VMEM((N_PAIRS, LANES, LANES), F32), pltpu.VMEM((1, RWKV_SLAB), F32),
                        pltpu.VMEM((n, 2 * LANES, LANES), BF16), pltpu.VMEM((n, LANES, LANES), F32),
                        pltpu.VMEM((n, LANES, LANES), F32)],
        compiler_params=_cparams(2), name="rwkv7")(p_rwkv.reshape(b, s, RWKV_SLAB), *params)


def _rwkv_params(mu, w0, w2, a0, a2, g2, k_k, k_a, r_k, gn_g, gn_b):
    row = lambda v: v.reshape(1, -1)
    z = jnp.zeros((RWKV_W_LORA, RWKV_W), F32)
    wwa = jnp.concatenate([jnp.concatenate([w2, z], axis=1), jnp.concatenate([z, a2], axis=1)], axis=0)
    return (row(mu), wwa.astype(BF16), row(w0), row(a0), g2.astype(BF16), row(k_k), row(k_a), row(r_k), row(gn_g), row(gn_b))


def _lru_kernel(p_ref, cw_ref, cb_ref, wax_ref, bax_ref, lam_ref, o_ref, xs_ref, h_ref):
    tb = p_ref.shape[0]
    w = LRU_W

    @pl.when(pl.program_id(1) == 0)
    def _():
        xs_ref[0:SUBLANES, :] = jnp.zeros((SUBLANES, w), F32)
        h_ref[...] = jnp.zeros(h_ref.shape, F32)

    x = p_ref[:, 0:w]
    gate = p_ref[:, w:]
    xs_ref[SUBLANES:, :] = x
    xl = x * cw_ref[LRU_CONV - 1:LRU_CONV, :] + cb_ref[...]
    for d in range(1, LRU_CONV):
        xl = xl + xs_ref[pl.ds(SUBLANES - d, tb), :] * cw_ref[LRU_CONV - 1 - d:LRU_CONV - d, :]
    xs_ref[0:SUBLANES, :] = x[tb - SUBLANES:, :]
    gates = _dot_bf(xl, wax_ref[...]) + bax_ref[...]
    r_gate = _sigmoid(gates[:, :w])
    i_gate = _sigmoid(gates[:, w:])
    log_a = -LRU_C * r_gate * _softplus(-lam_ref[...])
    a = jnp.exp(log_a)
    bv = jnp.sqrt((1.0 - a) * (1.0 + a)) * (i_gate * xl)
    rowi = lax.broadcasted_iota(I32, (tb, 1), 0)
    d = 1
    while d < tb:
        keep = rowi >= d
        a_sh = jnp.where(keep, pltpu.roll(a, d, 0), 1.0)
        b_sh = jnp.where(keep, pltpu.roll(bv, d, 0), 0.0)
        bv = a * b_sh + bv
        a = a * a_sh
        d *= 2
    h = bv + a * h_ref[...]
    h_ref[...] = h[tb - 1:tb, :]
    o_ref[...] = h * _gelu_tanh(gate)


def _lru(p_lru, params, b, s):
    tb = min(512, s)
    full = lambda a: pl.BlockSpec(a.shape, lambda i, j: (0,) * a.ndim)
    return pl.pallas_call(
        _lru_kernel, grid=(b, s // tb),
        in_specs=[pl.BlockSpec((None, tb, LRU_SLAB), lambda i, j: (i, j, 0))] + [full(a) for a in params],
        out_specs=pl.BlockSpec((None, tb, LRU_W), lambda i, j: (i, j, 0)),
        out_shape=jax.ShapeDtypeStruct((b, s, LRU_W), F32),
        scratch_shapes=[pltpu.VMEM((tb + SUBLANES, LRU_W), F32), pltpu.VMEM((1, LRU_W), F32)],
        compiler_params=_cparams(2), name="rglru")(p_lru.reshape(b, s, LRU_SLAB), *params)


def _lru_params(conv_w, conv_b, wa, ba, wx, bx, lam):
    row = lambda v: v.reshape(1, -1)
    def bd(m):
        out = jnp.zeros((LRU_W, LRU_W), F32)
        for i in range(LRU_H):
            out = out.at[i * HEAD_DIM:(i + 1) * HEAD_DIM, i * HEAD_DIM:(i + 1) * HEAD_DIM].set(m[i])
        return out
    wax = jnp.concatenate([bd(wa), bd(wx)], axis=1)
    return (conv_w, row(conv_b), wax.astype(BF16), jnp.concatenate([row(ba), row(bx)], axis=1), row(lam))


ROW_TILE = D_MODEL // LANES


def _rows_from_tiles(ref, n, lead=()):
    return jnp.concatenate([ref[lead + (pl.ds(k, n, stride=ROW_TILE), slice(None))] for k in range(ROW_TILE)], axis=-1)


def _rows_to_tiles(ref, x):
    for k in range(ROW_TILE):
        ref[pl.ds(k, x.shape[0], stride=ROW_TILE), :] = x[:, k * LANES:(k + 1) * LANES]


def _outproj_kernel(yr_ref, yw_ref, yl_ref, h_ref, w_ref, g_ref, b_ref, o_ref, ot_ref):
    ycat = jnp.concatenate([yr_ref[...], yw_ref[...], yl_ref[...]], axis=-1).astype(BF16)
    mix = jnp.dot(ycat, w_ref[...], preferred_element_type=F32)
    h1 = _layer_norm(ALPHA * h_ref[...] + mix, g_ref[...], b_ref[...])
    o_ref[...] = h1
    _rows_to_tiles(ot_ref, h1)


def _out_projection(y_ret, y_rwkv, y_lru, h, w_bf, g, b):
    t = h.shape[0]
    tm = min(512, t)
    row = lambda w: pl.BlockSpec((tm, w), lambda i: (i, 0))
    full = lambda a: pl.BlockSpec(a.shape, lambda i: (0,) * a.ndim)
    return pl.pallas_call(
        _outproj_kernel, grid=(t // tm,),
        in_specs=[row(RET_W), row(RWKV_W), row(LRU_W), row(D_MODEL), full(w_bf), full(g), full(b)],
        out_specs=[row(D_MODEL), pl.BlockSpec((tm * ROW_TILE, LANES), lambda i: (i, 0))],
        out_shape=[jax.ShapeDtypeStruct((t, D_MODEL), F32), jax.ShapeDtypeStruct((t * ROW_TILE, LANES), F32)],
        compiler_params=_cparams(1), name="out_proj_ln")(y_ret, y_rwkv, y_lru, h, w_bf, g, b)


def _router_kernel(h_ref, whi_ref, wlo_ref, b_ref, ew_ref, ei_ref, cnt_ref):
    @pl.when(pl.program_id(0) == 0)
    def _():
        cnt_ref[...] = jnp.zeros(cnt_ref.shape, F32)

    h = h_ref[...]
    h_hi = h.astype(BF16)
    h_lo = (h - h_hi.astype(F32)).astype(BF16)
    logits = (jnp.dot(h_hi, whi_ref[...], preferred_element_type=F32) + jnp.dot(h_lo, whi_ref[...], preferred_element_type=F32)
              + jnp.dot(h_hi, wlo_ref[...], preferred_element_type=F32)) + b_ref[...]
    lane = lax.broadcasted_iota(I32, logits.shape, 1)
    neg = -jnp.inf
    gmask = lane < N_GROUPS
    gl = jnp.where(gmask, logits, neg)
    gmax = jnp.max(gl, -1, keepdims=True)
    g_sel = jnp.min(jnp.where(gl == gmax, lane, LANES), -1, keepdims=True)
    g_gate = 1.0 / jnp.sum(jnp.where(gmask, jnp.exp(gl - gmax), 0.0), -1, keepdims=True)
    emask = (lane >= N_GROUPS) & (lane < N_GROUPS + N_EXPERTS) & (((lane - N_GROUPS) >> GROUP_SHIFT) == g_sel)
    el = jnp.where(emask, logits, neg)
    emax = jnp.max(el, -1, keepdims=True)
    ee = jnp.where(emask, jnp.exp(el - emax), 0.0)
    pe = ee / jnp.sum(ee, -1, keepdims=True)
    pe = jnp.where(emask, pe, -1.0)
    p1 = jnp.max(pe, -1, keepdims=True)
    i1 = jnp.min(jnp.where(pe == p1, lane, LANES), -1, keepdims=True)
    pe2 = jnp.where(lane == i1, -1.0, pe)
    p2 = jnp.max(pe2, -1, keepdims=True)
    i2 = jnp.min(jnp.where(pe2 == p2, lane, LANES), -1, keepdims=True)
    tot = p1 + p2
    ew_ref[...] = jnp.where(lane == 0, g_gate * (p1 / tot), jnp.where(lane == 1, g_gate * (p2 / tot), 0.0))
    tm = logits.shape[0]
    chosen = jnp.where((lane == i1) | (lane == i2), 1.0, 0.0)
    before = (lax.broadcasted_iota(I32, (tm, tm), 0) > lax.broadcasted_iota(I32, (tm, tm), 1))
    cum = jnp.dot(jnp.where(before, 1.0, 0.0).astype(BF16), chosen.astype(BF16), preferred_element_type=F32) + cnt_ref[0:1, :]
    r1 = jnp.sum(jnp.where(lane == i1, cum, 0.0), -1, keepdims=True).astype(I32)
    r2 = jnp.sum(jnp.where(lane == i2, cum, 0.0), -1, keepdims=True).astype(I32)
    cnt_ref[...] = jnp.broadcast_to(cnt_ref[0:1, :] + jnp.sum(chosen, 0, keepdims=True), cnt_ref.shape)
    ei_ref[...] = jnp.where(lane == 0, i1 - N_GROUPS, jnp.where(lane == 1, i2 - N_GROUPS,
                            jnp.where(lane == 2, r1, jnp.where(lane == 3, r2, 0))))


def _router(h, w_r, b_r):
    t = h.shape[0]
    tm = min(512, t)
    w_hi = w_r.astype(BF16)
    w_lo = (w_r - w_hi.astype(F32)).astype(BF16)
    row = lambda w: pl.BlockSpec((tm, w), lambda i: (i, 0))
    full = lambda a: pl.BlockSpec(a.shape, lambda i: (0,) * a.ndim)
    return pl.pallas_call(
        _router_kernel, grid=(t // tm,), in_specs=[row(D_MODEL), full(w_hi), full(w_lo), full(b_r)],
        out_specs=[row(LANES), row(LANES), pl.BlockSpec((SUBLANES, LANES), lambda i: (0, 0))],
        out_shape=[jax.ShapeDtypeStruct((t, LANES), F32), jax.ShapeDtypeStruct((t, LANES), I32),
                   jax.ShapeDtypeStruct((SUBLANES, LANES), F32)],
        compiler_params=_cparams(1), name="router")(h, w_hi, w_lo, b_r)


DMA_UNROLL = 8


def _tile_copy(src, s_row, dst, d_row, sem):
    return pltpu.make_async_copy(src.at[pl.ds(pl.multiple_of(s_row * ROW_TILE, ROW_TILE), ROW_TILE)],
                                 dst.at[pl.ds(pl.multiple_of(d_row * ROW_TILE, ROW_TILE), ROW_TILE)], sem)


def _expert_kernel(tok_ref, be_ref, nu_ref, ht_ref, wg_ref, wu_ref, wd_ref, o_ref, xbuf, sems, wg_s, wu_s, wd_s):
    i = pl.program_id(0)
    n_used = nu_ref[0]
    used = i < n_used
    new_expert = (i == 0) | (be_ref[i] != be_ref[jnp.maximum(i - 1, 0)])

    def gather(step, slot, fn):
        def body(r, c):
            fn(_tile_copy(ht_ref, tok_ref[step * MOE_BLOCK + r], xbuf.at[slot], r, sems.at[slot]))
            return c
        lax.fori_loop(0, MOE_BLOCK, body, 0, unroll=DMA_UNROLL)

    @pl.when(i == 0)
    def _():
        gather(0, 0, lambda cp: cp.start())

    @pl.when(i + 1 < n_used)
    def _():
        gather(i + 1, (i + 1) & 1, lambda cp: cp.start())

    @pl.when(used & new_expert)
    def _():
        wg_s[...] = wg_ref[...].astype(BF16)
        wu_s[...] = wu_ref[...].astype(BF16)
        wd_s[...] = wd_ref[...].astype(BF16)

    @pl.when(used)
    def _():
        slot = i & 1
        gather(i, slot, lambda cp: cp.wait())
        x = _rows_from_tiles(xbuf, MOE_BLOCK, (slot,)).astype(BF16)
        hid = _silu(jnp.dot(x, wg_s[...], preferred_element_type=F32)) * jnp.dot(x, wu_s[...], preferred_element_type=F32)
        _rows_to_tiles(o_ref, jnp.dot(hid.astype(BF16), wd_s[...], preferred_element_type=F32))

    @pl.when(jnp.logical_not(used))
    def _():
        o_ref[...] = jnp.zeros(o_ref.shape, F32)


def _experts(h_tiles, slot_tok, block_exp, n_used, wg, wu, wd, layer):
    n_blocks = block_exp.shape[0]
    w_spec = lambda r, c: pl.BlockSpec((None, None, r, c), lambda i, tk, be, nu: (layer, be[i], 0, 0))
    return pl.pallas_call(
        _expert_kernel,
        grid_spec=pltpu.PrefetchScalarGridSpec(
            num_scalar_prefetch=3, grid=(n_blocks,),
            in_specs=[pl.BlockSpec(memory_space=pl.ANY),
                      w_spec(D_MODEL, D_EXPERT), w_spec(D_MODEL, D_EXPERT), w_spec(D_EXPERT, D_MODEL)],
            out_specs=pl.BlockSpec((MOE_BLOCK * ROW_TILE, LANES), lambda i, tk, be, nu: (i, 0)),
            scratch_shapes=[pltpu.VMEM((2, MOE_BLOCK * ROW_TILE, LANES), F32), pltpu.SemaphoreType.DMA((2,)),
                            pltpu.VMEM((D_MODEL, D_EXPERT), BF16), pltpu.VMEM((D_MODEL, D_EXPERT), BF16),
                            pltpu.VMEM((D_EXPERT, D_MODEL), BF16)]),
        out_shape=jax.ShapeDtypeStruct((n_blocks * MOE_BLOCK * ROW_TILE, LANES), F32),
        compiler_params=_cparams(1), name="moe_experts")(slot_tok, block_exp, n_used, h_tiles, wg, wu, wd)


def _combine_kernel(dest_ref, yb_ref, ew_ref, h_ref, g_ref, b_ref, o_ref, buf, sems):
    tc = h_ref.shape[0]
    i = pl.program_id(0)
    n = pl.num_programs(0)

    def copies(step, slot, fn):
        def body(r, c):
            tok = step * tc + r
            for j in range(2):
                fn(_tile_copy(yb_ref, dest_ref[2 * tok + j], buf.at[slot, j], r, sems.at[slot]))
            return c
        lax.fori_loop(0, tc, body, 0, unroll=DMA_UNROLL)

    @pl.when(i == 0)
    def _():
        copies(0, 0, lambda cp: cp.start())

    @pl.when(i + 1 < n)
    def _():
        copies(i + 1, (i + 1) & 1, lambda cp: cp.start())

    slot = i & 1
    copies(i, slot, lambda cp: cp.wait())
    ew = ew_ref[...]
    y = ew[:, 0:1] * _rows_from_tiles(buf, tc, (slot, 0)) + ew[:, 1:2] * _rows_from_tiles(buf, tc, (slot, 1))
    o_ref[...] = _layer_norm(ALPHA * h_ref[...] + y, g_ref[...], b_ref[...])


def _combine(yb_tiles, dest, ew, h, g, b):
    t = h.shape[0]
    tc = min(256, t)
    row = lambda w: pl.BlockSpec((tc, w), lambda i, d: (i, 0))
    full = lambda a: pl.BlockSpec(a.shape, lambda i, d: (0,) * a.ndim)
    return pl.pallas_call(
        _combine_kernel,
        grid_spec=pltpu.PrefetchScalarGridSpec(
            num_scalar_prefetch=1, grid=(t // tc,),
            in_specs=[pl.BlockSpec(memory_space=pl.ANY), row(LANES), row(D_MODEL), full(g), full(b)],
            out_specs=row(D_MODEL),
            scratch_shapes=[pltpu.VMEM((2, 2, tc * ROW_TILE, LANES), F32), pltpu.SemaphoreType.DMA((2,))]),
        out_shape=jax.ShapeDtypeStruct((t, D_MODEL), F32),
        compiler_params=_cparams(1), name="moe_combine_ln")(dest, yb_tiles, ew, h, g, b)


def _moe_plan(ei, cnt, t):
    a = t * 2
    counts = cnt[0, N_GROUPS:N_GROUPS + N_EXPERTS].astype(I32)
    padded = ((counts + MOE_BLOCK - 1) // MOE_BLOCK) * MOE_BLOCK
    pends = jnp.cumsum(padded)
    pstarts = pends - padded
    e_ids = jnp.arange(N_EXPERTS, dtype=I32)[None, None, :]
    start_of = jnp.sum(jnp.where(ei[:, 0:2, None] == e_ids, pstarts[None, None, :], 0), axis=-1)
    e_flat = ei[:, 0:2].reshape(a)
    dest = (start_of + ei[:, 2:4]).reshape(a).astype(I32)
    n_blocks = -(-a // MOE_BLOCK) + N_EXPERTS
    block_row0 = jnp.arange(n_blocks, dtype=I32) * MOE_BLOCK
    block_exp = jnp.minimum(jnp.sum((pends[None, :] <= block_row0[:, None]).astype(I32), axis=1), N_EXPERTS - 1)
    n_used = (pends[-1:] // MOE_BLOCK).astype(I32)
    tok_sorted = jnp.argsort(e_flat, stable=True).astype(I32) >> 1
    in_expert = block_row0 - pstarts[block_exp]
    src = (jnp.cumsum(counts) - counts)[block_exp] + in_expert
    lane_r = jnp.arange(MOE_BLOCK, dtype=I32)[None, :]
    valid = (in_expert[:, None] + lane_r) < counts[block_exp][:, None]
    slot_tok = jnp.where(valid, tok_sorted[jnp.clip(src[:, None] + lane_r, 0, a - 1)], 0).reshape(-1)
    return dest, slot_tok.astype(I32), block_exp.astype(I32), n_used


def _hier_moe_ln(h, h_tiles, w_r, b_r, wg, wu, wd, layer, g, b):
    t = h.shape[0]
    ew, ei, cnt = _router(h, w_r, b_r)
    dest, slot_tok, block_exp, n_used = _moe_plan(ei, cnt, t)
    yb_tiles = _experts(h_tiles, slot_tok, block_exp, n_used, wg, wu, wd, layer)
    return _combine(yb_tiles, dest, ew, h, g, b)


def kernel(x, ln_in_g, ln_in_b, w_in, w_out, rwkv_mu, rwkv_w0, rwkv_w2, rwkv_a0, rwkv_a2, rwkv_g2, rwkv_k_k, rwkv_k_a, rwkv_r_k, rwkv_gn_g, rwkv_gn_b, lru_conv_w, lru_conv_b, lru_wa, lru_ba, lru_wx, lru_bx, lru_lambda, ln1_g, ln1_b, moe_wg, moe_bg, moe_we, moe_be, moe_w_gate, moe_w_up, moe_w_down, ln2_g, ln2_b):
    b, s, d = x.shape
    t = b * s
    row = lambda v: v.reshape(1, -1)
    ret_consts = _retention_consts(s)
    h = x.reshape(t, d)
    for l in range(DEPTH):
        w_in_bf = w_in[l].astype(BF16)
        if l == 0:
            h, p_ret, p_rwkv, p_lru = _in_projection(h, w_in_bf, ln=(row(ln_in_g), row(ln_in_b)))
        else:
            p_ret, p_rwkv, p_lru = _in_projection(h, w_in_bf)
        y_ret = _retention(p_ret, ret_consts, b, s)
        y_rwkv = _rwkv(p_rwkv, _rwkv_params(rwkv_mu[l], rwkv_w0[l], rwkv_w2[l], rwkv_a0[l], rwkv_a2[l], rwkv_g2[l],
                                            rwkv_k_k[l], rwkv_k_a[l], rwkv_r_k[l], rwkv_gn_g[l], rwkv_gn_b[l]), b, s)
        y_lru = _lru(p_lru, _lru_params(lru_conv_w[l], lru_conv_b[l], lru_wa[l], lru_ba[l], lru_wx[l], lru_bx[l],
                                        lru_lambda[l]), b, s)
        h, h_tiles = _out_projection(y_ret.reshape(t, RET_W), y_rwkv.reshape(t, RWKV_W), y_lru.reshape(t, LRU_W), h,
                                     w_out[l].astype(BF16), row(ln1_g[l]), row(ln1_b[l]))
        pad = jnp.zeros((d, LANES - N_GROUPS - N_EXPERTS), F32)
        w_r = jnp.concatenate([moe_wg[l], moe_we[l], pad], axis=1)
        b_r = jnp.concatenate([moe_bg[l], moe_be[l], jnp.zeros((LANES - N_GROUPS - N_EXPERTS,), F32)]).reshape(1, LANES)
        h = _hier_moe_ln(h, h_tiles, w_r, b_r, moe_w_gate, moe_w_up, moe_w_down, l, row(ln2_g[l]), row(ln2_b[l]))
    return h.reshape(b, s, d)
```

```python
import functools

import jax
import jax.numpy as jnp
from jax import lax
from jax.experimental import pallas as pl
from jax.experimental.pallas import tpu as pltpu

F32 = jnp.float32
BF16 = jnp.bfloat16
I32 = jnp.int32

D_MODEL = 1024
DEPTH = 2
HEAD_DIM = 64
HEAD_SHIFT = 6
RET_H = 6
RET_W = RET_H * HEAD_DIM
RET_CHUNK = 128
ROPE_BASE = 10000.0
RWKV_H = 6
RWKV_W = RWKV_H * HEAD_DIM
RWKV_W_LORA = 64
RWKV_A_LORA = 64
RWKV_G_LORA = 128
RWKV_GN_EPS = 64e-5
LRU_H = 4
LRU_W = LRU_H * HEAD_DIM
LRU_CONV = 4
LRU_C = 8.0
RET_SLAB = 4 * RET_W
RWKV_SLAB = 3 * RWKV_W + RWKV_W_LORA + RWKV_A_LORA + RWKV_G_LORA
LRU_SLAB = 2 * LRU_W
IN_W = RET_SLAB + RWKV_SLAB + LRU_SLAB
N_GROUPS = 4
EXPERTS_PER_GROUP = 8
GROUP_SHIFT = 3
N_EXPERTS = N_GROUPS * EXPERTS_PER_GROUP
D_EXPERT = D_MODEL // 2
MOE_BLOCK = 256
LN_EPS = 1e-5
ALPHA = (2.0 * DEPTH) ** 0.25

LANES = 128
SUBLANES = 8
N_PAIRS = RET_W // LANES
RWKV_CHUNK = 64
RWKV_CHUNK_SHIFT = 6
VMEM_LIMIT = 56 * 1024 * 1024


def _cparams(n_axes, vmem=VMEM_LIMIT):
    return pltpu.CompilerParams(dimension_semantics=("arbitrary",) * n_axes, vmem_limit_bytes=vmem)


def _dot_bf(a, b):
    return jnp.dot(a.astype(BF16), b.astype(BF16), preferred_element_type=F32)


def _split_dot(a, b, passes, split_lhs):
    rem = a if split_lhs else b
    acc = None
    for _ in range(passes):
        piece = rem.astype(BF16)
        part = jnp.dot(piece, b, preferred_element_type=F32) if split_lhs else jnp.dot(a, piece, preferred_element_type=F32)
        acc = part if acc is None else acc + part
        rem = rem - piece.astype(F32)
    return acc


def _bmm(a, b):
    return jnp.einsum('nij,njk->nik', a.astype(BF16), b.astype(BF16), preferred_element_type=F32)


def _sigmoid(x):
    return 1.0 / (1.0 + jnp.exp(-x))


def _silu(x):
    return x * _sigmoid(x)


def _softplus(x):
    return jnp.maximum(x, 0.0) + jnp.log1p(jnp.exp(-jnp.abs(x)))


def _gelu_tanh(x):
    return 0.5 * x * (1.0 + jnp.tanh(0.7978845608028654 * (x + 0.044715 * (x * x * x))))


def _layer_norm(x, g, b):
    mu = jnp.mean(x, -1, keepdims=True)
    xc = x - mu
    var = jnp.mean(xc * xc, -1, keepdims=True)
    return xc * lax.rsqrt(var + LN_EPS) * g + b


def _pair_masks():
    lane = lax.broadcasted_iota(I32, (1, LANES), 1)
    m_a = lane < HEAD_DIM
    row = lax.broadcasted_iota(I32, (LANES, LANES), 0)
    col = lax.broadcasted_iota(I32, (LANES, LANES), 1)
    same_head = (row >> HEAD_SHIFT) == (col >> HEAD_SHIFT)
    return m_a, row, col, same_head


def _head_norm_pair(y, same_head, eps):
    avg = jnp.where(same_head, 1.0 / HEAD_DIM, 0.0).astype(BF16)
    mu = _split_dot(y, avg, 2, True)
    yc = y - mu
    var = _split_dot(yc * yc, avg, 2, True)
    return yc * lax.rsqrt(var + eps)


def _inproj_ln_kernel(x_ref, g_ref, b_ref, w_ref, h_ref, pr_ref, pw_ref, pl_ref):
    h = _layer_norm(x_ref[...], g_ref[...], b_ref[...])
    h_ref[...] = h
    p = jnp.dot(h.astype(BF16), w_ref[...], preferred_element_type=F32)
    pr_ref[...] = p[:, :RET_SLAB]
    pw_ref[...] = p[:, RET_SLAB:RET_SLAB + RWKV_SLAB]
    pl_ref[...] = p[:, RET_SLAB + RWKV_SLAB:]


def _inproj_kernel(h_ref, w_ref, pr_ref, pw_ref, pl_ref):
    p = jnp.dot(h_ref[...].astype(BF16), w_ref[...], preferred_element_type=F32)
    pr_ref[...] = p[:, :RET_SLAB]
    pw_ref[...] = p[:, RET_SLAB:RET_SLAB + RWKV_SLAB]
    pl_ref[...] = p[:, RET_SLAB + RWKV_SLAB:]


def _in_projection(x2d, w_bf, ln=None):
    t = x2d.shape[0]
    tm = min(512, t)
    row = lambda w: pl.BlockSpec((tm, w), lambda i: (i, 0))
    full = lambda a: pl.BlockSpec(a.shape, lambda i: (0,) * a.ndim)
    outs = [jax.ShapeDtypeStruct((t, RET_SLAB), F32), jax.ShapeDtypeStruct((t, RWKV_SLAB), F32),
            jax.ShapeDtypeStruct((t, LRU_SLAB), F32)]
    out_specs = [row(RET_SLAB), row(RWKV_SLAB), row(LRU_SLAB)]
    if ln is None:
        return pl.pallas_call(
            _inproj_kernel, grid=(t // tm,), in_specs=[row(D_MODEL), full(w_bf)], out_specs=out_specs,
            out_shape=outs, compiler_params=_cparams(1), name="in_proj")(x2d, w_bf)
    g, b = ln
    return pl.pallas_call(
        _inproj_ln_kernel, grid=(t // tm,), in_specs=[row(D_MODEL), full(g), full(b), full(w_bf)],
        out_specs=[row(D_MODEL)] + out_specs, out_shape=[jax.ShapeDtypeStruct((t, D_MODEL), F32)] + outs,
        compiler_params=_cparams(1), name="ln_in_proj")(x2d, g, b, w_bf)


def _swap_halves(t):
    n = t.shape[-1]
    lane = lax.broadcasted_iota(I32, (1, n), 1)
    lower = (lane & (HEAD_DIM - 1)) < (HEAD_DIM // 2)
    return jnp.where(lower, pltpu.roll(t, n - HEAD_DIM // 2, 1), pltpu.roll(t, HEAD_DIM // 2, 1))


def _retention_kernel(p_ref, cos_ref, sin_ref, dmat_ref, qw_ref, kw_ref, gch_ref, o_ref, st_ref):
    @pl.when(pl.program_id(1) == 0)
    def _():
        st_ref[...] = jnp.zeros(st_ref.shape, F32)

    tb = p_ref.shape[0]
    c = RET_CHUNK
    n_ch = tb // c
    pairs = [slice(pi * LANES, (pi + 1) * LANES) for pi in range(N_PAIRS)]
    m_a, _, _, same_head = _pair_masks()
    m_b = jnp.logical_not(m_a)
    cos = jnp.concatenate([cos_ref[...]] * N_PAIRS, axis=-1)
    sin = jnp.concatenate([sin_ref[...]] * N_PAIRS, axis=-1)
    q = p_ref[:, 0:RET_W]
    k = p_ref[:, RET_W:2 * RET_W]
    v = p_ref[:, 2 * RET_W:3 * RET_W]
    q = q * cos + _swap_halves(q) * sin
    k = (k * cos + _swap_halves(k) * sin) * (HEAD_DIM ** -0.5)

    def per_problem(fn):
        return jnp.stack([fn(slice(ci * c, (ci + 1) * c), pi) for ci in range(n_ch) for pi in range(N_PAIRS)])

    def stack(x):
        return jnp.concatenate([jnp.where(m_a, x, 0.0), jnp.where(m_b, x, 0.0)], axis=0)

    k_b = per_problem(lambda rows, pi: k[rows, pairs[pi]]).astype(BF16)
    v_b = per_problem(lambda rows, pi: v[rows, pairs[pi]]).astype(BF16)
    q_st = per_problem(lambda rows, pi: stack(q[rows, pairs[pi]])).astype(BF16)
    v_st = per_problem(lambda rows, pi: stack(v[rows, pairs[pi]])).astype(BF16)
    q_w = per_problem(lambda rows, pi: q[rows, pairs[pi]] * qw_ref[:, pairs[pi]]).astype(BF16)
    kw_t = jnp.swapaxes(per_problem(lambda rows, pi: k[rows, pairs[pi]] * kw_ref[:, pairs[pi]]), 1, 2).astype(BF16)
    decay = per_problem(lambda rows, pi: jnp.concatenate([dmat_ref[2 * pi], dmat_ref[2 * pi + 1]], axis=0))

    s = jnp.einsum('nik,njk->nij', q_st, k_b, preferred_element_type=F32) * decay
    intra = _bmm(jnp.concatenate([s[:, :c], s[:, c:]], axis=2), v_st)
    kv = _bmm(kw_t, v_b)
    states = []
    for ci in range(n_ch):
        for pi in range(N_PAIRS):
            st = st_ref[pi]
            states.append(st)
            st_ref[pi] = st * gch_ref[:, pairs[pi]] + jnp.where(same_head, kv[ci * N_PAIRS + pi], 0.0)
    y_all = intra + _bmm(q_w, jnp.stack(states))
    y = jnp.concatenate([jnp.concatenate([y_all[ci * N_PAIRS + pi] for pi in range(N_PAIRS)], axis=1)
                         for ci in range(n_ch)], axis=0)
    yn = jnp.concatenate([_head_norm_pair(y[:, ln], same_head, LN_EPS) for ln in pairs], axis=1)
    o_ref[...] = yn * _silu(p_ref[:, 3 * RET_W:4 * RET_W])


def _retention(p_ret, consts, b, s):
    tb = min(512, s)
    cos_t, sin_t, dmat, qw, kw, gch = consts
    full = lambda a: pl.BlockSpec(a.shape, lambda i, j: (0,) * a.ndim)
    return pl.pallas_call(
        _retention_kernel, grid=(b, s // tb),
        in_specs=[pl.BlockSpec((None, tb, RET_SLAB), lambda i, j: (i, j, 0)),
                  pl.BlockSpec((tb, LANES), lambda i, j: (j, 0)), pl.BlockSpec((tb, LANES), lambda i, j: (j, 0)),
                  full(dmat), full(qw), full(kw), full(gch)],
        out_specs=pl.BlockSpec((None, tb, RET_W), lambda i, j: (i, j, 0)),
        out_shape=jax.ShapeDtypeStruct((b, s, RET_W), F32),
        scratch_shapes=[pltpu.VMEM((N_PAIRS, LANES, LANES), F32)],
        compiler_params=_cparams(2), name="retention")(p_ret.reshape(b, s, RET_SLAB), cos_t, sin_t, dmat, qw, kw, gch)


def _retention_consts(s):
    half = HEAD_DIM // 2
    inv = 1.0 / (ROPE_BASE ** (jnp.arange(0, HEAD_DIM, 2, dtype=F32) / HEAD_DIM))
    ang = jnp.arange(s, dtype=F32)[:, None] * inv[None, :]
    cos, sin = jnp.cos(ang), jnp.sin(ang)
    cos_t = jnp.tile(cos, (1, LANES // half))
    sin_t = jnp.tile(jnp.concatenate([-sin, sin], axis=-1), (1, LANES // HEAD_DIM))
    c = RET_CHUNK
    log_g = jnp.log1p(-jnp.exp2(-5.0 - jnp.arange(RET_H, dtype=F32)))
    pos = jnp.arange(c, dtype=F32)
    diff = pos[:, None] - pos[None, :]
    causal = diff >= 0
    dmat = jnp.where(causal[None], jnp.exp(jnp.where(causal, diff, 0.0)[None] * log_g[:, None, None]), 0.0)
    lane_g = jnp.repeat(log_g, HEAD_DIM)[None, :]
    qw = jnp.exp((pos + 1.0)[:, None] * lane_g)
    kw = jnp.exp((c - 1.0 - pos)[:, None] * lane_g)
    gch = jnp.exp(c * lane_g)
    return cos_t, sin_t, dmat, qw, kw, gch


def _rwkv_kernel(p_ref, mu_ref, wwa_ref, w0_ref, a0_ref, g2_ref, kk_ref, ka_ref, rk_ref, gng_ref, gnb_ref,
                 o_ref, st_ref, carry_ref, rm_s, g_s, y0_s):
    tb = p_ref.shape[0]
    c = RWKV_CHUNK
    w = RWKV_W
    n_ch = tb // c
    pairs = [slice(pi * LANES, (pi + 1) * LANES) for pi in range(N_PAIRS)]

    @pl.when(pl.program_id(1) == 0)
    def _():
        st_ref[...] = jnp.zeros(st_ref.shape, F32)
        carry_ref[...] = jnp.zeros(carry_ref.shape, F32)

    m_a, row, col, same_head = _pair_masks()
    m_b = jnp.logical_not(m_a)
    head_ones = jnp.where(same_head, 1.0, 0.0).astype(BF16)

    def head_sums(x):
        return jnp.concatenate([_split_dot(x[:, ln], head_ones, 2, True) for ln in pairs], axis=1)

    p = p_ref[...]
    rowi = lax.broadcasted_iota(I32, (tb, 1), 0)
    prev = jnp.where(rowi == 0, carry_ref[...], pltpu.roll(p, 1, 0))
    carry_ref[...] = p[tb - 1:tb, :]
    z = p + (prev - p) * mu_ref[...]
    r = z[:, 0:w]
    kr = z[:, w:2 * w]
    vr = z[:, 2 * w:3 * w]
    wa = z[:, 3 * w:3 * w + LANES]
    gl = z[:, 3 * w + LANES:]
    lane = lax.broadcasted_iota(I32, (1, LANES), 1)
    pre = _dot_bf(jnp.where(lane < RWKV_W_LORA, jnp.tanh(wa), wa), wwa_ref[...])
    w_log = -_softplus(-(w0_ref[...] + pre[:, :w])) - 0.5
    a = _sigmoid(a0_ref[...] + pre[:, w:])
    g_rw = _dot_bf(_sigmoid(gl), g2_ref[...])
    kk = kr * kk_ref[...]
    kk = kk / jnp.maximum(jnp.sqrt(head_sums(kk * kk)), 1e-12)
    kmod = kr * (1.0 + (a - 1.0) * ka_ref[...])
    bonus = head_sums(r * kmod * rk_ref[...]) * vr
    ld = -jnp.exp(w_log)

    t_r = lax.broadcasted_iota(I32, (tb, tb), 0)
    t_c = lax.broadcasted_iota(I32, (tb, tb), 1)
    same_chunk = (t_r >> RWKV_CHUNK_SHIFT) == (t_c >> RWKV_CHUNK_SHIFT)
    lc = _split_dot(jnp.where(same_chunk & (t_r >= t_c), 1.0, 0.0).astype(BF16), ld, 3, False)
    lc_end = _split_dot(jnp.where(same_chunk, 1.0, 0.0).astype(BF16), ld, 3, False)
    p_inv = jnp.exp(-lc)
    p_rem = jnp.exp(lc_end - lc)
    p_end = jnp.exp(lc_end)
    b0 = kk * a

    def batch(x):
        return jnp.stack([jnp.concatenate([jnp.where(m_a, x[ci * c:(ci + 1) * c, ln], 0.0),
                                           jnp.where(m_b, x[ci * c:(ci + 1) * c, ln], 0.0)], axis=0)
                          for ci in range(n_ch) for ln in pairs])

    lr = batch(r * jnp.exp(lc))
    la_b = batch(-kk * jnp.exp(lc - ld)).astype(BF16)
    lr_b = lr.astype(BF16)
    rb_b = batch(b0 * p_inv).astype(BF16)
    rk_b = batch(kmod * p_inv).astype(BF16)
    vs_b = batch(vr).astype(BF16)
    be_t = jnp.swapaxes(batch(b0 * p_rem), 1, 2)
    ke_t = jnp.swapaxes(batch(kmod * p_rem), 1, 2)
    pe = jnp.stack([p_end[ci * c:ci * c + 1, ln] for ci in range(n_ch) for ln in pairs])

    tri_incl = (row >= col)[None]
    tri_strict = (row > col)[None]
    eye = (row == col)[None]
    xx = jnp.einsum('nik,njk->nij', jnp.concatenate([la_b, lr_b], axis=1), jnp.concatenate([rb_b, rk_b], axis=1),
                    preferred_element_type=F32)
    x_ab = jnp.where(tri_strict, xx[:, :LANES, :LANES], 0.0)
    x_ak = jnp.where(tri_strict, xx[:, :LANES, LANES:], 0.0)
    x_rb = jnp.where(tri_incl, xx[:, LANES:, :LANES], 0.0)
    x_rk = jnp.where(tri_incl, xx[:, LANES:, LANES:], 0.0)
    inv = None
    for sh in range(RWKV_CHUNK_SHIFT):
        lv = (((row >> (sh + 1)) == (col >> (sh + 1))) & (((row >> sh) & 1) == 1) & (((col >> sh) & 1) == 0))[None]
        x_lv = jnp.where(lv, x_ab, 0.0)
        inv = jnp.where(eye, 1.0, 0.0) + x_lv if inv is None else inv + _bmm(_bmm(inv, x_lv), inv)
    wy = _bmm(jnp.concatenate([x_ak, x_rk], axis=1), vs_b)
    aw_b = _bmm(inv, jnp.concatenate([la_b, wy[:, :LANES].astype(BF16)], axis=2)).astype(BF16)
    ba = _bmm(be_t, aw_b)
    m_mat = jnp.where(eye, pe, 0.0) + ba[:, :, :LANES]
    ra = _bmm(x_rb, aw_b)
    rm_s[...] = jnp.concatenate([lr + ra[:, :, :LANES], m_mat], axis=1).astype(BF16)
    g_s[...] = ba[:, :, LANES:] + _bmm(ke_t, vs_b)
    y0_s[...] = ra[:, :, LANES:] + wy[:, LANES:]

    st = [st_ref[pi] for pi in range(N_PAIRS)]
    y_rows = []
    for ci in range(n_ch):
        ys = []
        for pi in range(N_PAIRS):
            n = ci * N_PAIRS + pi
            sy = jnp.dot(rm_s[n], st[pi].astype(BF16), preferred_element_type=F32)
            yst = sy[:LANES] + y0_s[n]
            st[pi] = sy[LANES:] + g_s[n]
            ys.append(yst[:c] + yst[c:])
        y_rows.append(jnp.concatenate(ys, axis=1))
    for pi in range(N_PAIRS):
        st_ref[pi] = st[pi]
    y = jnp.concatenate(y_rows, axis=0)

    yn = jnp.concatenate([_head_norm_pair(y[:, ln], same_head, RWKV_GN_EPS) for ln in pairs], axis=1)
    o_ref[...] = (yn * gng_ref[...] + gnb_ref[...] + bonus) * g_rw


def _rwkv(p_rwkv, params, b, s):
    tb = min(256, s)
    n = (tb // RWKV_CHUNK) * N_PAIRS
    full = lambda a: pl.BlockSpec(a.shape, lambda i, j: (0,) * a.ndim)
    return pl.pallas_call(
        _rwkv_kernel, grid=(b, s // tb),
        in_specs=[pl.BlockSpec((None, tb, RWKV_SLAB), lambda i, j: (i, j, 0))] + [full(a) for a in params],
        out_specs=pl.BlockSpec((None, tb, RWKV_W), lambda i, j: (i, j, 0)),
        out_shape=jax.ShapeDtypeStruct((b, s, RWKV_W), F32),
        scratch_shapes=[pltpu.VMEM((N_PAIRS, LANES, LANES), F32), pltpu.VMEM((1, RWKV_SLAB), F32),
                        pltpu.VMEM((n, 2 * LANES, LANES), BF16), pltpu.VMEM((n, LANES, LANES), F32),
                        pltpu.VMEM((n, LANES, LANES), F32)],
        compiler_params=_cparams(2), name="rwkv7")(p_rwkv.reshape(b, s, RWKV_SLAB), *params)


def _rwkv_params(mu, w0, w2, a0, a2, g2, k_k, k_a, r_k, gn_g, gn_b):
    row = lambda v: v.reshape(1, -1)
    z = jnp.zeros((RWKV_W_LORA, RWKV_W), F32)
    wwa = jnp.concatenate([jnp.concatenate([w2, z], axis=1), jnp.concatenate([z, a2], axis=1)], axis=0)
    return (row(mu), wwa.astype(BF16), row(w0), row(a0), g2.astype(BF16), row(k_k), row(k_a), row(r_k), row(gn_g), row(gn_b))


def _lru_kernel(p_ref, cw_ref, cb_ref, wax_ref, bax_ref, lam_ref, o_ref, xs_ref, h_ref):
    tb = p_ref.shape[0]
    w = LRU_W

    @pl.when(pl.program_id(1) == 0)
    def _():
        xs_ref[0:SUBLANES, :] = jnp.zeros((SUBLANES, w), F32)
        h_ref[...] = jnp.zeros(h_ref.shape, F32)

    x = p_ref[:, 0:w]
    gate = p_ref[:, w:]
    xs_ref[SUBLANES:, :] = x
    xl = x * cw_ref[LRU_CONV - 1:LRU_CONV, :] + cb_ref[...]
    for d in range(1, LRU_CONV):
        xl = xl + xs_ref[pl.ds(SUBLANES - d, tb), :] * cw_ref[LRU_CONV - 1 - d:LRU_CONV - d, :]
    xs_ref[0:SUBLANES, :] = x[tb - SUBLANES:, :]
    gates = _dot_bf(xl, wax_ref[...]) + bax_ref[...]
    r_gate = _sigmoid(gates[:, :w])
    i_gate = _sigmoid(gates[:, w:])
    log_a = -LRU_C * r_gate * _softplus(-lam_ref[...])
    a = jnp.exp(log_a)
    bv = jnp.sqrt((1.0 - a) * (1.0 + a)) * (i_gate * xl)
    rowi = lax.broadcasted_iota(I32, (tb, 1), 0)
    d = 1
    while d < tb:
        keep = rowi >= d
        a_sh = jnp.where(keep, pltpu.roll(a, d, 0), 1.0)
        b_sh = jnp.where(keep, pltpu.roll(bv, d, 0), 0.0)
        bv = a * b_sh + bv
        a = a * a_sh
        d *= 2
    h = bv + a * h_ref[...]
    h_ref[...] = h[tb - 1:tb, :]
    o_ref[...] = h * _gelu_tanh(gate)


def _lru(p_lru, params, b, s):
    tb = min(512, s)
    full = lambda a: pl.BlockSpec(a.shape, lambda i, j: (0,) * a.ndim)
    return pl.pallas_call(
        _lru_kernel, grid=(b, s // tb),
        in_specs=[pl.BlockSpec((None, tb, LRU_SLAB), lambda i, j: (i, j, 0))] + [full(a) for a in params],
        out_specs=pl.BlockSpec((None, tb, LRU_W), lambda i, j: (i, j, 0)),
        out_shape=jax.ShapeDtypeStruct((b, s, LRU_W), F32),
        scratch_shapes=[pltpu.VMEM((tb + SUBLANES, LRU_W), F32), pltpu.VMEM((1, LRU_W), F32)],
        compiler_params=_cparams(2), name="rglru")(p_lru.reshape(b, s, LRU_SLAB), *params)


def _lru_params(conv_w, conv_b, wa, ba, wx, bx, lam):
    row = lambda v: v.reshape(1, -1)
    def bd(m):
        out = jnp.zeros((LRU_W, LRU_W), F32)
        for i in range(LRU_H):
            out = out.at[i * HEAD_DIM:(i + 1) * HEAD_DIM, i * HEAD_DIM:(i + 1) * HEAD_DIM].set(m[i])
        return out
    wax = jnp.concatenate([bd(wa), bd(wx)], axis=1)
    return (conv_w, row(conv_b), wax.astype(BF16), jnp.concatenate([row(ba), row(bx)], axis=1), row(lam))


ROW_TILE = D_MODEL // LANES


def _rows_from_tiles(ref, n, lead=(), first=0):
    return jnp.concatenate([ref[lead + (pl.ds(first * ROW_TILE + k, n, stride=ROW_TILE), slice(None))]
                            for k in range(ROW_TILE)], axis=-1)


def _rows_to_tiles(ref, x):
    for k in range(ROW_TILE):
        ref[pl.ds(k, x.shape[0], stride=ROW_TILE), :] = x[:, k * LANES:(k + 1) * LANES]


def _outproj_kernel(yr_ref, yw_ref, yl_ref, h_ref, w_ref, g_ref, b_ref, o_ref, ot_ref):
    ycat = jnp.concatenate([yr_ref[...], yw_ref[...], yl_ref[...]], axis=-1).astype(BF16)
    mix = jnp.dot(ycat, w_ref[...], preferred_element_type=F32)
    h1 = _layer_norm(ALPHA * h_ref[...] + mix, g_ref[...], b_ref[...])
    o_ref[...] = h1
    _rows_to_tiles(ot_ref, h1)


def _out_projection(y_ret, y_rwkv, y_lru, h, w_bf, g, b):
    t = h.shape[0]
    tm = min(512, t)
    row = lambda w: pl.BlockSpec((tm, w), lambda i: (i, 0))
    full = lambda a: pl.BlockSpec(a.shape, lambda i: (0,) * a.ndim)
    return pl.pallas_call(
        _outproj_kernel, grid=(t // tm,),
        in_specs=[row(RET_W), row(RWKV_W), row(LRU_W), row(D_MODEL), full(w_bf), full(g), full(b)],
        out_specs=[row(D_MODEL), pl.BlockSpec((tm * ROW_TILE, LANES), lambda i: (i, 0))],
        out_shape=[jax.ShapeDtypeStruct((t, D_MODEL), F32), jax.ShapeDtypeStruct((t * ROW_TILE, LANES), F32)],
        compiler_params=_cparams(1), name="out_proj_ln")(y_ret, y_rwkv, y_lru, h, w_bf, g, b)


def _router_kernel(h_ref, whi_ref, wlo_ref, b_ref, ew_ref, ei_ref, cnt_ref):
    @pl.when(pl.program_id(0) == 0)
    def _():
        cnt_ref[...] = jnp.zeros(cnt_ref.shape, F32)

    h = h_ref[...]
    h_hi = h.astype(BF16)
    h_lo = (h - h_hi.astype(F32)).astype(BF16)
    logits = (jnp.dot(h_hi, whi_ref[...], preferred_element_type=F32) + jnp.dot(h_lo, whi_ref[...], preferred_element_type=F32)
              + jnp.dot(h_hi, wlo_ref[...], preferred_element_type=F32)) + b_ref[...]
    lane = lax.broadcasted_iota(I32, logits.shape, 1)
    neg = -jnp.inf
    gmask = lane < N_GROUPS
    gl = jnp.where(gmask, logits, neg)
    gmax = jnp.max(gl, -1, keepdims=True)
    g_sel = jnp.min(jnp.where(gl == gmax, lane, LANES), -1, keepdims=True)
    g_gate = 1.0 / jnp.sum(jnp.where(gmask, jnp.exp(gl - gmax), 0.0), -1, keepdims=True)
    emask = (lane >= N_GROUPS) & (lane < N_GROUPS + N_EXPERTS) & (((lane - N_GROUPS) >> GROUP_SHIFT) == g_sel)
    el = jnp.where(emask, logits, neg)
    emax = jnp.max(el, -1, keepdims=True)
    ee = jnp.where(emask, jnp.exp(el - emax), 0.0)
    pe = ee / jnp.sum(ee, -1, keepdims=True)
    pe = jnp.where(emask, pe, -1.0)
    p1 = jnp.max(pe, -1, keepdims=True)
    i1 = jnp.min(jnp.where(pe == p1, lane, LANES), -1, keepdims=True)
    pe2 = jnp.where(lane == i1, -1.0, pe)
    p2 = jnp.max(pe2, -1, keepdims=True)
    i2 = jnp.min(jnp.where(pe2 == p2, lane, LANES), -1, keepdims=True)
    tot = p1 + p2
    ew_ref[...] = jnp.where(lane == 0, g_gate * (p1 / tot), jnp.where(lane == 1, g_gate * (p2 / tot), 0.0))
    tm = logits.shape[0]
    chosen = jnp.where((lane == i1) | (lane == i2), 1.0, 0.0)
    before = (lax.broadcasted_iota(I32, (tm, tm), 0) > lax.broadcasted_iota(I32, (tm, tm), 1))
    cum = jnp.dot(jnp.where(before, 1.0, 0.0).astype(BF16), chosen.astype(BF16), preferred_element_type=F32) + cnt_ref[0:1, :]
    r1 = jnp.sum(jnp.where(lane == i1, cum, 0.0), -1, keepdims=True).astype(I32)
    r2 = jnp.sum(jnp.where(lane == i2, cum, 0.0), -1, keepdims=True).astype(I32)
    cnt_ref[...] = jnp.broadcast_to(cnt_ref[0:1, :] + jnp.sum(chosen, 0, keepdims=True), cnt_ref.shape)
    ei_ref[...] = jnp.where(lane == 0, i1 - N_GROUPS, jnp.where(lane == 1, i2 - N_GROUPS,
                            jnp.where(lane == 2, r1, jnp.where(lane == 3, r2, 0))))


def _router(h, w_r, b_r):
    t = h.shape[0]
    tm = min(512, t)
    w_hi = w_r.astype(BF16)
    w_lo = (w_r - w_hi.astype(F32)).astype(BF16)
    row = lambda w: pl.BlockSpec((tm, w), lambda i: (i, 0))
    full = lambda a: pl.BlockSpec(a.shape, lambda i: (0,) * a.ndim)
    return pl.pallas_call(
        _router_kernel, grid=(t // tm,), in_specs=[row(D_MODEL), full(w_hi), full(w_lo), full(b_r)],
        out_specs=[row(LANES), row(LANES), pl.BlockSpec((SUBLANES, LANES), lambda i: (0, 0))],
        out_shape=[jax.ShapeDtypeStruct((t, LANES), F32), jax.ShapeDtypeStruct((t, LANES), I32),
                   jax.ShapeDtypeStruct((SUBLANES, LANES), F32)],
        compiler_params=_cparams(1), name="router")(h, w_hi, w_lo, b_r)


DMA_UNROLL = 8
EXPERT_PHASES = 2


def _tile_copy(src, s_row, dst, d_row, sem):
    return pltpu.make_async_copy(src.at[pl.ds(pl.multiple_of(s_row * ROW_TILE, ROW_TILE), ROW_TILE)],
                                 dst.at[pl.ds(pl.multiple_of(d_row * ROW_TILE, ROW_TILE), ROW_TILE)], sem)


def _expert_kernel(tok_ref, be_ref, nu_ref, ht_ref, wg_ref, wu_ref, wd_ref, o_ref, xbuf, sems, wg_s, wu_s, wd_s):
    i = pl.program_id(0)
    n_used = nu_ref[0]
    used = i < n_used
    new_expert = (i == 0) | (be_ref[i] != be_ref[jnp.maximum(i - 1, 0)])

    def row_copy(step, slot, r):
        return _tile_copy(ht_ref, tok_ref[step * MOE_BLOCK + r], xbuf.at[slot], r, sems.at[slot])

    def gather_loop(step, slot, fn):
        def body(r, c):
            fn(row_copy(step, slot, r))
            return c
        lax.fori_loop(0, MOE_BLOCK, body, 0, unroll=DMA_UNROLL)

    @pl.when(i == 0)
    def _():
        gather_loop(0, 0, lambda cp: cp.start())

    @pl.when(used & new_expert)
    def _():
        wg_s[...] = wg_ref[...].astype(BF16)
        wu_s[...] = wu_ref[...].astype(BF16)
        wd_s[...] = wd_ref[...].astype(BF16)

    @pl.when(used)
    def _():
        slot = i & 1
        gather_loop(i, slot, lambda cp: cp.wait())
        x = _rows_from_tiles(xbuf, MOE_BLOCK, (slot,)).astype(BF16)
        acc = None
        tile = D_EXPERT // EXPERT_PHASES
        rows_per_phase = MOE_BLOCK // EXPERT_PHASES
        for j in range(EXPERT_PHASES):
            cols = slice(j * tile, (j + 1) * tile)
            hid = _silu(jnp.dot(x, wg_s[:, cols], preferred_element_type=F32)) * jnp.dot(x, wu_s[:, cols], preferred_element_type=F32)
            part = jnp.dot(hid.astype(BF16), wd_s[cols, :], preferred_element_type=F32)
            acc = part if acc is None else acc + part
            for r in range(j * rows_per_phase, (j + 1) * rows_per_phase):
                row_copy(i + 1, 1 - slot, r).start()
        _rows_to_tiles(o_ref, acc)

    @pl.when(i == n_used)
    def _():
        gather_loop(i, i & 1, lambda cp: cp.wait())

    @pl.when(jnp.logical_not(used))
    def _():
        o_ref[...] = jnp.zeros(o_ref.shape, F32)


def _experts(h_tiles, slot_tok, block_exp, n_used, wg, wu, wd, layer):
    n_blocks = block_exp.shape[0]
    w_spec = lambda r, c: pl.BlockSpec((None, None, r, c), lambda i, tk, be, nu: (layer, be[i], 0, 0))
    return pl.pallas_call(
        _expert_kernel,
        grid_spec=pltpu.PrefetchScalarGridSpec(
            num_scalar_prefetch=3, grid=(n_blocks,),
            in_specs=[pl.BlockSpec(memory_space=pl.ANY),
                      w_spec(D_MODEL, D_EXPERT), w_spec(D_MODEL, D_EXPERT), w_spec(D_EXPERT, D_MODEL)],
            out_specs=pl.BlockSpec((MOE_BLOCK * ROW_TILE, LANES), lambda i, tk, be, nu: (i, 0)),
            scratch_shapes=[pltpu.VMEM((2, MOE_BLOCK * ROW_TILE, LANES), F32), pltpu.SemaphoreType.DMA((2,)),
                            pltpu.VMEM((D_MODEL, D_EXPERT), BF16), pltpu.VMEM((D_MODEL, D_EXPERT), BF16),
                            pltpu.VMEM((D_EXPERT, D_MODEL), BF16)]),
        out_shape=jax.ShapeDtypeStruct((n_blocks * MOE_BLOCK * ROW_TILE, LANES), F32),
        compiler_params=_cparams(1), name="moe_experts")(slot_tok, block_exp, n_used, h_tiles, wg, wu, wd)


COMBINE_PHASES = 4


def _combine_kernel(dest_ref, yb_ref, ew_ref, h_ref, g_ref, b_ref, o_ref, buf, sems, *, n_real):
    tc = h_ref.shape[0]
    i = pl.program_id(0)

    def row_copy(step, slot, r, j):
        return _tile_copy(yb_ref, dest_ref[2 * (step * tc + r) + j], buf.at[slot, j], r, sems.at[slot])

    def copy_loop(step, slot, fn):
        def body(r, c):
            fn(row_copy(step, slot, r, 0))
            fn(row_copy(step, slot, r, 1))
            return c
        lax.fori_loop(0, tc, body, 0, unroll=DMA_UNROLL)

    @pl.when(i == 0)
    def _():
        copy_loop(0, 0, lambda cp: cp.start())

    @pl.when(i < n_real)
    def _():
        slot = i & 1
        nxt = jnp.minimum(i + 1, n_real - 1)
        copy_loop(i, slot, lambda cp: cp.wait())
        rows = tc // COMBINE_PHASES
        for ph in range(COMBINE_PHASES):
            rs = slice(ph * rows, (ph + 1) * rows)
            ew = ew_ref[rs, :]
            y = (ew[:, 0:1] * _rows_from_tiles(buf, rows, (slot, 0), ph * rows)
                 + ew[:, 1:2] * _rows_from_tiles(buf, rows, (slot, 1), ph * rows))
            o_ref[rs, :] = _layer_norm(ALPHA * h_ref[rs, :] + y, g_ref[...], b_ref[...])
            for r in range(ph * rows, (ph + 1) * rows):
                row_copy(nxt, 1 - slot, r, 0).start()
                row_copy(nxt, 1 - slot, r, 1).start()

    @pl.when(i == n_real)
    def _():
        copy_loop(n_real - 1, i & 1, lambda cp: cp.wait())


def _combine(yb_tiles, dest, ew, h, g, b):
    t = h.shape[0]
    tc = min(256, t)
    n_real = t // tc
    row = lambda w: pl.BlockSpec((tc, w), lambda i, d: (jnp.minimum(i, n_real - 1), 0))
    full = lambda a: pl.BlockSpec(a.shape, lambda i, d: (0,) * a.ndim)
    return pl.pallas_call(
        functools.partial(_combine_kernel, n_real=n_real),
        grid_spec=pltpu.PrefetchScalarGridSpec(
            num_scalar_prefetch=1, grid=(n_real + 1,),
            in_specs=[pl.BlockSpec(memory_space=pl.ANY), row(LANES), row(D_MODEL), full(g), full(b)],
            out_specs=row(D_MODEL),
            scratch_shapes=[pltpu.VMEM((2, 2, tc * ROW_TILE, LANES), F32), pltpu.SemaphoreType.DMA((2,))]),
        out_shape=jax.ShapeDtypeStruct((t, D_MODEL), F32),
        compiler_params=_cparams(1), name="moe_combine_ln")(dest, yb_tiles, ew, h, g, b)


def _moe_plan(ei, cnt, t):
    a = t * 2
    counts = cnt[0, N_GROUPS:N_GROUPS + N_EXPERTS].astype(I32)
    padded = ((counts + MOE_BLOCK - 1) // MOE_BLOCK) * MOE_BLOCK
    pends = jnp.cumsum(padded)
    pstarts = pends - padded
    e_ids = jnp.arange(N_EXPERTS, dtype=I32)[None, None, :]
    start_of = jnp.sum(jnp.where(ei[:, 0:2, None] == e_ids, pstarts[None, None, :], 0), axis=-1)
    e_flat = ei[:, 0:2].reshape(a)
    dest = (start_of + ei[:, 2:4]).reshape(a).astype(I32)
    n_blocks = -(-a // MOE_BLOCK) + N_EXPERTS
    block_row0 = jnp.arange(n_blocks, dtype=I32) * MOE_BLOCK
    block_exp = jnp.minimum(jnp.sum((pends[None, :] <= block_row0[:, None]).astype(I32), axis=1), N_EXPERTS - 1)
    n_used = (pends[-1:] // MOE_BLOCK).astype(I32)
    tok_sorted = jnp.argsort(e_flat, stable=True).astype(I32) >> 1
    in_expert = block_row0 - pstarts[block_exp]
    src = (jnp.cumsum(counts) - counts)[block_exp] + in_expert
    lane_r = jnp.arange(MOE_BLOCK, dtype=I32)[None, :]
    valid = (in_expert[:, None] + lane_r) < counts[block_exp][:, None]
    slot_tok = jnp.where(valid, tok_sorted[jnp.clip(src[:, None] + lane_r, 0, a - 1)], 0).reshape(-1)
    return dest, slot_tok.astype(I32), block_exp.astype(I32), n_used


def _hier_moe_ln(h, h_tiles, w_r, b_r, wg, wu, wd, layer, g, b):
    t = h.shape[0]
    ew, ei, cnt = _router(h, w_r, b_r)
    dest, slot_tok, block_exp, n_used = _moe_plan(ei, cnt, t)
    yb_tiles = _experts(h_tiles, slot_tok, block_exp, n_used, wg, wu, wd, layer)
    return _combine(yb_tiles, dest, ew, h, g, b)


def kernel(x, ln_in_g, ln_in_b, w_in, w_out, rwkv_mu, rwkv_w0, rwkv_w2, rwkv_a0, rwkv_a2, rwkv_g2, rwkv_k_k, rwkv_k_a, rwkv_r_k, rwkv_gn_g, rwkv_gn_b, lru_conv_w, lru_conv_b, lru_wa, lru_ba, lru_wx, lru_bx, lru_lambda, ln1_g, ln1_b, moe_wg, moe_bg, moe_we, moe_be, moe_w_gate, moe_w_up, moe_w_down, ln2_g, ln2_b):
    b, s, d = x.shape
    t = b * s
    row = lambda v: v.reshape(1, -1)
    ret_consts = _retention_consts(s)
    h = x.reshape(t, d)
    for l in range(DEPTH):
        w_in_bf = w_in[l].astype(BF16)
        if l == 0:
            h, p_ret, p_rwkv, p_lru = _in_projection(h, w_in_bf, ln=(row(ln_in_g), row(ln_in_b)))
        else:
            p_ret, p_rwkv, p_lru = _in_projection(h, w_in_bf)
        y_ret = _retention(p_ret, ret_consts, b, s)
        y_rwkv = _rwkv(p_rwkv, _rwkv_params(rwkv_mu[l], rwkv_w0[l], rwkv_w2[l], rwkv_a0[l], rwkv_a2[l], rwkv_g2[l],
                                            rwkv_k_k[l], rwkv_k_a[l], rwkv_r_k[l], rwkv_gn_g[l], rwkv_gn_b[l]), b, s)
        y_lru = _lru(p_lru, _lru_params(lru_conv_w[l], lru_conv_b[l], lru_wa[l], lru_ba[l], lru_wx[l], lru_bx[l],
                                        lru_lambda[l]), b, s)
        h, h_tiles = _out_projection(y_ret.reshape(t, RET_W), y_rwkv.reshape(t, RWKV_W), y_lru.reshape(t, LRU_W), h,
                                     w_out[l].astype(BF16), row(ln1_g[l]), row(ln1_b[l]))
        pad = jnp.zeros((d, LANES - N_GROUPS - N_EXPERTS), F32)
        w_r = jnp.concatenate([moe_wg[l], moe_we[l], pad], axis=1)
        b_r = jnp.concatenate([moe_bg[l], moe_be[l], jnp.zeros((LANES - N_GROUPS - N_EXPERTS,), F32)]).reshape(1, LANES)
        h = _hier_moe_ln(h, h_tiles, w_r, b_r, moe_w_gate, moe_w_up, moe_w_down, l, row(ln2_g[l]), row(ln2_b[l]))
    return h.reshape(b, s, d)
```

```python
import jax
import jax.numpy as jnp
from jax import lax
from jax.experimental import pallas as pl
from jax.experimental.pallas import tpu as pltpu

F32 = jnp.float32
BF16 = jnp.bfloat16
I32 = jnp.int32

D_MODEL = 1024
DEPTH = 2
HEAD_DIM = 64
HEAD_SHIFT = 6
RET_H = 6
RET_W = RET_H * HEAD_DIM
RET_CHUNK = 128
ROPE_BASE = 10000.0
RWKV_H = 6
RWKV_W = RWKV_H * HEAD_DIM
RWKV_W_LORA = 64
RWKV_A_LORA = 64
RWKV_G_LORA = 128
RWKV_GN_EPS = 64e-5
LRU_H = 4
LRU_W = LRU_H * HEAD_DIM
LRU_CONV = 4
LRU_C = 8.0
RET_SLAB = 4 * RET_W
RWKV_SLAB = 3 * RWKV_W + RWKV_W_LORA + RWKV_A_LORA + RWKV_G_LORA
LRU_SLAB = 2 * LRU_W
IN_W = RET_SLAB + RWKV_SLAB + LRU_SLAB
N_GROUPS = 4
EXPERTS_PER_GROUP = 8
GROUP_SHIFT = 3
N_EXPERTS = N_GROUPS * EXPERTS_PER_GROUP
D_EXPERT = D_MODEL // 2
MOE_BLOCK = 256
LN_EPS = 1e-5
ALPHA = (2.0 * DEPTH) ** 0.25

LANES = 128
SUBLANES = 8
N_PAIRS = RET_W // LANES
RWKV_CHUNK = 64
RWKV_CHUNK_SHIFT = 6
VMEM_LIMIT = 56 * 1024 * 1024


def _cparams(n_axes, vmem=VMEM_LIMIT):
    return pltpu.CompilerParams(dimension_semantics=("arbitrary",) * n_axes, vmem_limit_bytes=vmem)


def _dot_bf(a, b):
    return jnp.dot(a.astype(BF16), b.astype(BF16), preferred_element_type=F32)


def _split_dot(a, b, passes, split_lhs):
    rem = a if split_lhs else b
    acc = None
    for _ in range(passes):
        piece = rem.astype(BF16)
        part = jnp.dot(piece, b, preferred_element_type=F32) if split_lhs else jnp.dot(a, piece, preferred_element_type=F32)
        acc = part if acc is None else acc + part
        rem = rem - piece.astype(F32)
    return acc


def _bmm(a, b):
    return jnp.einsum('nij,njk->nik', a.astype(BF16), b.astype(BF16), preferred_element_type=F32)


def _sigmoid(x):
    return 1.0 / (1.0 + jnp.exp(-x))


def _silu(x):
    return x * _sigmoid(x)


def _softplus(x):
    return jnp.maximum(x, 0.0) + jnp.log1p(jnp.exp(-jnp.abs(x)))


def _gelu_tanh(x):
    return 0.5 * x * (1.0 + jnp.tanh(0.7978845608028654 * (x + 0.044715 * (x * x * x))))


def _layer_norm(x, g, b):
    mu = jnp.mean(x, -1, keepdims=True)
    xc = x - mu
    var = jnp.mean(xc * xc, -1, keepdims=True)
    return xc * lax.rsqrt(var + LN_EPS) * g + b


def _pair_masks():
    lane = lax.broadcasted_iota(I32, (1, LANES), 1)
    m_a = lane < HEAD_DIM
    row = lax.broadcasted_iota(I32, (LANES, LANES), 0)
    col = lax.broadcasted_iota(I32, (LANES, LANES), 1)
    same_head = (row >> HEAD_SHIFT) == (col >> HEAD_SHIFT)
    return m_a, row, col, same_head


def _head_norm_pair(y, same_head, eps):
    avg = jnp.where(same_head, 1.0 / HEAD_DIM, 0.0).astype(BF16)
    mu = _split_dot(y, avg, 2, True)
    yc = y - mu
    var = _split_dot(yc * yc, avg, 2, True)
    return yc * lax.rsqrt(var + eps)


def _inproj_ln_kernel(x_ref, g_ref, b_ref, w_ref, h_ref, pr_ref, pw_ref, pl_ref):
    h = _layer_norm(x_ref[...], g_ref[...], b_ref[...])
    h_ref[...] = h
    p = jnp.dot(h.astype(BF16), w_ref[...], preferred_element_type=F32)
    pr_ref[...] = p[:, :RET_SLAB]
    pw_ref[...] = p[:, RET_SLAB:RET_SLAB + RWKV_SLAB]
    pl_ref[...] = p[:, RET_SLAB + RWKV_SLAB:]


def _inproj_kernel(h_ref, w_ref, pr_ref, pw_ref, pl_ref):
    p = jnp.dot(h_ref[...].astype(BF16), w_ref[...], preferred_element_type=F32)
    pr_ref[...] = p[:, :RET_SLAB]
    pw_ref[...] = p[:, RET_SLAB:RET_SLAB + RWKV_SLAB]
    pl_ref[...] = p[:, RET_SLAB + RWKV_SLAB:]


def _in_projection(x2d, w_bf, ln=None):
    t = x2d.shape[0]
    tm = min(512, t)
    row = lambda w: pl.BlockSpec((tm, w), lambda i: (i, 0))
    full = lambda a: pl.BlockSpec(a.shape, lambda i: (0,) * a.ndim)
    outs = [jax.ShapeDtypeStruct((t, RET_SLAB), F32), jax.ShapeDtypeStruct((t, RWKV_SLAB), F32),
            jax.ShapeDtypeStruct((t, LRU_SLAB), F32)]
    out_specs = [row(RET_SLAB), row(RWKV_SLAB), row(LRU_SLAB)]
    if ln is None:
        return pl.pallas_call(
            _inproj_kernel, grid=(t // tm,), in_specs=[row(D_MODEL), full(w_bf)], out_specs=out_specs,
            out_shape=outs, compiler_params=_cparams(1), name="in_proj")(x2d, w_bf)
    g, b = ln
    return pl.pallas_call(
        _inproj_ln_kernel, grid=(t // tm,), in_specs=[row(D_MODEL), full(g), full(b), full(w_bf)],
        out_specs=[row(D_MODEL)] + out_specs, out_shape=[jax.ShapeDtypeStruct((t, D_MODEL), F32)] + outs,
        compiler_params=_cparams(1), name="ln_in_proj")(x2d, g, b, w_bf)


def _swap_halves(t):
    n = t.shape[-1]
    lane = lax.broadcasted_iota(I32, (1, n), 1)
    lower = (lane & (HEAD_DIM - 1)) < (HEAD_DIM // 2)
    return jnp.where(lower, pltpu.roll(t, n - HEAD_DIM // 2, 1), pltpu.roll(t, HEAD_DIM // 2, 1))


def _retention_kernel(p_ref, cos_ref, sin_ref, dmat_ref, qw_ref, kw_ref, gch_ref, o_ref, st_ref):
    @pl.when(pl.program_id(1) == 0)
    def _():
        st_ref[...] = jnp.zeros(st_ref.shape, F32)

    tb = p_ref.shape[0]
    c = RET_CHUNK
    n_ch = tb // c
    pairs = [slice(pi * LANES, (pi + 1) * LANES) for pi in range(N_PAIRS)]
    m_a, _, _, same_head = _pair_masks()
    m_b = jnp.logical_not(m_a)
    cos = jnp.concatenate([cos_ref[...]] * N_PAIRS, axis=-1)
    sin = jnp.concatenate([sin_ref[...]] * N_PAIRS, axis=-1)
    q = p_ref[:, 0:RET_W]
    k = p_ref[:, RET_W:2 * RET_W]
    v = p_ref[:, 2 * RET_W:3 * RET_W]
    q = q * cos + _swap_halves(q) * sin
    k = (k * cos + _swap_halves(k) * sin) * (HEAD_DIM ** -0.5)

    def per_problem(fn):
        return jnp.stack([fn(slice(ci * c, (ci + 1) * c), pi) for ci in range(n_ch) for pi in range(N_PAIRS)])

    def stack(x):
        return jnp.concatenate([jnp.where(m_a, x, 0.0), jnp.where(m_b, x, 0.0)], axis=0)

    k_b = per_problem(lambda rows, pi: k[rows, pairs[pi]]).astype(BF16)
    v_b = per_problem(lambda rows, pi: v[rows, pairs[pi]]).astype(BF16)
    q_st = per_problem(lambda rows, pi: stack(q[rows, pairs[pi]])).astype(BF16)
    v_st = per_problem(lambda rows, pi: stack(v[rows, pairs[pi]])).astype(BF16)
    q_w = per_problem(lambda rows, pi: q[rows, pairs[pi]] * qw_ref[:, pairs[pi]]).astype(BF16)
    kw_t = jnp.swapaxes(per_problem(lambda rows, pi: k[rows, pairs[pi]] * kw_ref[:, pairs[pi]]), 1, 2).astype(BF16)
    decay = per_problem(lambda rows, pi: jnp.concatenate([dmat_ref[2 * pi], dmat_ref[2 * pi + 1]], axis=0))

    s = jnp.einsum('nik,njk->nij', q_st, k_b, preferred_element_type=F32) * decay
    intra = _bmm(jnp.concatenate([s[:, :c], s[:, c:]], axis=2), v_st)
    kv = _bmm(kw_t, v_b)
    states = []
    for ci in range(n_ch):
        for pi in range(N_PAIRS):
            st = st_ref[pi]
            states.append(st)
            st_ref[pi] = st * gch_ref[:, pairs[pi]] + jnp.where(same_head, kv[ci * N_PAIRS + pi], 0.0)
    y_all = intra + _bmm(q_w, jnp.stack(states))
    y = jnp.concatenate([jnp.concatenate([y_all[ci * N_PAIRS + pi] for pi in range(N_PAIRS)], axis=1)
                         for ci in range(n_ch)], axis=0)
    yn = jnp.concatenate([_head_norm_pair(y[:, ln], same_head, LN_EPS) for ln in pairs], axis=1)
    o_ref[...] = yn * _silu(p_ref[:, 3 * RET_W:4 * RET_W])


def _retention(p_ret, consts, b, s):
    tb = min(512, s)
    cos_t, sin_t, dmat, qw, kw, gch = consts
    full = lambda a: pl.BlockSpec(a.shape, lambda i, j: (0,) * a.ndim)
    return pl.pallas_call(
        _retention_kernel, grid=(b, s // tb),
        in_specs=[pl.BlockSpec((None, tb, RET_SLAB), lambda i, j: (i, j, 0)),
                  pl.BlockSpec((tb, LANES), lambda i, j: (j, 0)), pl.BlockSpec((tb, LANES), lambda i, j: (j, 0)),
                  full(dmat), full(qw), full(kw), full(gch)],
        out_specs=pl.BlockSpec((None, tb, RET_W), lambda i, j: (i, j, 0)),
        out_shape=jax.ShapeDtypeStruct((b, s, RET_W), F32),
        scratch_shapes=[pltpu.VMEM((N_PAIRS, LANES, LANES), F32)],
        compiler_params=_cparams(2), name="retention")(p_ret.reshape(b, s, RET_SLAB), cos_t, sin_t, dmat, qw, kw, gch)


def _retention_consts(s):
    half = HEAD_DIM // 2
    inv = 1.0 / (ROPE_BASE ** (jnp.arange(0, HEAD_DIM, 2, dtype=F32) / HEAD_DIM))
    ang = jnp.arange(s, dtype=F32)[:, None] * inv[None, :]
    cos, sin = jnp.cos(ang), jnp.sin(ang)
    cos_t = jnp.tile(cos, (1, LANES // half))
    sin_t = jnp.tile(jnp.concatenate([-sin, sin], axis=-1), (1, LANES // HEAD_DIM))
    c = RET_CHUNK
    log_g = jnp.log1p(-jnp.exp2(-5.0 - jnp.arange(RET_H, dtype=F32)))
    pos = jnp.arange(c, dtype=F32)
    diff = pos[:, None] - pos[None, :]
    causal = diff >= 0
    dmat = jnp.where(causal[None], jnp.exp(jnp.where(causal, diff, 0.0)[None] * log_g[:, None, None]), 0.0)
    lane_g = jnp.repeat(log_g, HEAD_DIM)[None, :]
    qw = jnp.exp((pos + 1.0)[:, None] * lane_g)
    kw = jnp.exp((c - 1.0 - pos)[:, None] * lane_g)
    gch = jnp.exp(c * lane_g)
    return cos_t, sin_t, dmat, qw, kw, gch


def _rwkv_kernel(p_ref, mu_ref, wwa_ref, w0_ref, a0_ref, g2_ref, kk_ref, ka_ref, rk_ref, gng_ref, gnb_ref,
                 o_ref, st_ref, carry_ref, rm_s, g_s, y0_s):
    tb = p_ref.shape[0]
    c = RWKV_CHUNK
    w = RWKV_W
    n_ch = tb // c
    pairs = [slice(pi * LANES, (pi + 1) * LANES) for pi in range(N_PAIRS)]

    @pl.when(pl.program_id(1) == 0)
    def _():
        st_ref[...] = jnp.zeros(st_ref.shape, F32)
        carry_ref[...] = jnp.zeros(carry_ref.shape, F32)

    m_a, row, col, same_head = _pair_masks()
    m_b = jnp.logical_not(m_a)
    head_ones = jnp.where(same_head, 1.0, 0.0).astype(BF16)

    def head_sums(x):
        return jnp.concatenate([_split_dot(x[:, ln], head_ones, 2, True) for ln in pairs], axis=1)

    p = p_ref[...]
    rowi = lax.broadcasted_iota(I32, (tb, 1), 0)
    prev = jnp.where(rowi == 0, carry_ref[...], pltpu.roll(p, 1, 0))
    carry_ref[...] = p[tb - 1:tb, :]
    z = p + (prev - p) * mu_ref[...]
    r = z[:, 0:w]
    kr = z[:, w:2 * w]
    vr = z[:, 2 * w:3 * w]
    wa = z[:, 3 * w:3 * w + LANES]
    gl = z[:, 3 * w + LANES:]
    lane = lax.broadcasted_iota(I32, (1, LANES), 1)
    pre = _dot_bf(jnp.where(lane < RWKV_W_LORA, jnp.tanh(wa), wa), wwa_ref[...])
    w_log = -_softplus(-(w0_ref[...] + pre[:, :w])) - 0.5
    a = _sigmoid(a0_ref[...] + pre[:, w:])
    g_rw = _dot_bf(_sigmoid(gl), g2_ref[...])
    kk = kr * kk_ref[...]
    kk = kk / jnp.maximum(jnp.sqrt(head_sums(kk * kk)), 1e-12)
    kmod = kr * (1.0 + (a - 1.0) * ka_ref[...])
    bonus = head_sums(r * kmod * rk_ref[...]) * vr
    ld = -jnp.exp(w_log)

    t_r = lax.broadcasted_iota(I32, (tb, tb), 0)
    t_c = lax.broadcasted_iota(I32, (tb, tb), 1)
    same_chunk = (t_r >> RWKV_CHUNK_SHIFT) == (t_c >> RWKV_CHUNK_SHIFT)
    lc = _split_dot(jnp.where(same_chunk & (t_r >= t_c), 1.0, 0.0).astype(BF16), ld, 3, False)
    lc_end = _split_dot(jnp.where(same_chunk, 1.0, 0.0).astype(BF16), ld, 3, False)
    p_inv = jnp.exp(-lc)
    p_rem = jnp.exp(lc_end - lc)
    p_end = jnp.exp(lc_end)
    b0 = kk * a

    def batch(x):
        return jnp.stack([jnp.concatenate([jnp.where(m_a, x[ci * c:(ci + 1) * c, ln], 0.0),
                                           jnp.where(m_b, x[ci * c:(ci + 1) * c, ln], 0.0)], axis=0)
                          for ci in range(n_ch) for ln in pairs])

    lr = batch(r * jnp.exp(lc))
    la_b = batch(-kk * jnp.exp(lc - ld)).astype(BF16)
    lr_b = lr.astype(BF16)
    rb_b = batch(b0 * p_inv).astype(BF16)
    rk_b = batch(kmod * p_inv).astype(BF16)
    vs_b = batch(vr).astype(BF16)
    be_t = jnp.swapaxes(batch(b0 * p_rem), 1, 2)
    ke_t = jnp.swapaxes(batch(kmod * p_rem), 1, 2)
    pe = jnp.stack([p_end[ci * c:ci * c + 1, ln] for ci in range(n_ch) for ln in pairs])

    tri_incl = (row >= col)[None]
    tri_strict = (row > col)[None]
    eye = (row == col)[None]
    xx = jnp.einsum('nik,njk->nij', jnp.concatenate([la_b, lr_b], axis=1), jnp.concatenate([rb_b, rk_b], axis=1),
                    preferred_element_type=F32)
    x_ab = jnp.where(tri_strict, xx[:, :LANES, :LANES], 0.0)
    x_ak = jnp.where(tri_strict, xx[:, :LANES, LANES:], 0.0)
    x_rb = jnp.where(tri_incl, xx[:, LANES:, :LANES], 0.0)
    x_rk = jnp.where(tri_incl, xx[:, LANES:, LANES:], 0.0)
    inv = None
    for sh in range(RWKV_CHUNK_SHIFT):
        lv = (((row >> (sh + 1)) == (col >> (sh + 1))) & (((row >> sh) & 1) == 1) & (((col >> sh) & 1) == 0))[None]
        x_lv = jnp.where(lv, x_ab, 0.0)
        inv = jnp.where(eye, 1.0, 0.0) + x_lv if inv is None else inv + _bmm(_bmm(inv, x_lv), inv)
    wy = _bmm(jnp.concatenate([x_ak, x_rk], axis=1), vs_b)
    aw_b = _bmm(inv, jnp.concatenate([la_b, wy[:, :LANES].astype(BF16)], axis=2)).astype(BF16)
    ba = _bmm(be_t, aw_b)
    m_mat = jnp.where(eye, pe, 0.0) + ba[:, :, :LANES]
    ra = _bmm(x_rb, aw_b)
    rm_s[...] = jnp.concatenate([lr + ra[:, :, :LANES], m_mat], axis=1).astype(BF16)
    g_s[...] = ba[:, :, LANES:] + _bmm(ke_t, vs_b)
    y0_s[...] = ra[:, :, LANES:] + wy[:, LANES:]

    st = [st_ref[pi] for pi in range(N_PAIRS)]
    y_rows = []
    for ci in range(n_ch):
        ys = []
        for pi in range(N_PAIRS):
            n = ci * N_PAIRS + pi
            sy = jnp.dot(rm_s[n], st[pi].astype(BF16), preferred_element_type=F32)
            yst = sy[:LANES] + y0_s[n]
            st[pi] = sy[LANES:] + g_s[n]
            ys.append(yst[:c] + yst[c:])
        y_rows.append(jnp.concatenate(ys, axis=1))
    for pi in range(N_PAIRS):
        st_ref[pi] = st[pi]
    y = jnp.concatenate(y_rows, axis=0)

    yn = jnp.concatenate([_head_norm_pair(y[:, ln], same_head, RWKV_GN_EPS) for ln in pairs], axis=1)
    o_ref[...] = (yn * gng_ref[...] + gnb_ref[...] + bonus) * g_rw


def _rwkv(p_rwkv, params, b, s):
    tb = min(256, s)
    n = (tb // RWKV_CHUNK) * N_PAIRS
    full = lambda a: pl.BlockSpec(a.shape, lambda i, j: (0,) * a.ndim)
    return pl.pallas_call(
        _rwkv_kernel, grid=(b, s // tb),
        in_specs=[pl.BlockSpec((None, tb, RWKV_SLAB), lambda i, j: (i, j, 0))] + [full(a) for a in params],
        out_specs=pl.BlockSpec((None, tb, RWKV_W), lambda i, j: (i, j, 0)),
        out_shape=jax.ShapeDtypeStruct((b, s, RWKV_W), F32),
        scratch_shapes=[pltpu.VMEM((N_PAIRS, LANES, LANES), F32), pltpu.VMEM((1, RWKV_SLAB), F32),
                        pltpu.VMEM((n, 2 * LANES, LANES), BF16), pltpu.VMEM((n, LANES, LANES), F32),
                        pltpu.VMEM((n, LANES, LANES), F32)],
        compiler_params=_cparams(2), name="rwkv7")(p_rwkv.reshape(b, s, RWKV_SLAB), *params)


def _rwkv_params(mu, w0, w2, a0, a2, g2, k_k, k_a, r_k, gn_g, gn_b):
    row = lambda v: v.reshape(1, -1)
    z = jnp.zeros((RWKV_W_LORA, RWKV_W), F32)
    wwa = jnp.concatenate([jnp.concatenate([w2, z], axis=1), jnp.concatenate([z, a2], axis=1)], axis=0)
    return (row(mu), wwa.astype(BF16), row(w0), row(a0), g2.astype(BF16), row(k_k), row(k_a), row(r_k), row(gn_g), row(gn_b))


def _lru_kernel(p_ref, cw_ref, cb_ref, wax_ref, bax_ref, lam_ref, o_ref, xs_ref, h_ref):
    tb = p_ref.shape[0]
    w = LRU_W

    @pl.when(pl.program_id(1) == 0)
    def _():
        xs_ref[0:SUBLANES, :] = jnp.zeros((SUBLANES, w), F32)
        h_ref[...] = jnp.zeros(h_ref.shape, F32)

    x = p_ref[:, 0:w]
    gate = p_ref[:, w:]
    xs_ref[SUBLANES:, :] = x
    xl = x * cw_ref[LRU_CONV - 1:LRU_CONV, :] + cb_ref[...]
    for d in range(1, LRU_CONV):
        xl = xl + xs_ref[pl.ds(SUBLANES - d, tb), :] * cw_ref[LRU_CONV - 1 - d:LRU_CONV - d, :]
    xs_ref[0:SUBLANES, :] = x[tb - SUBLANES:, :]
    gates = _dot_bf(xl, wax_ref[...]) + bax_ref[...]
    r_gate = _sigmoid(gates[:, :w])
    i_gate = _sigmoid(gates[:, w:])
    log_a = -LRU_C * r_gate * _softplus(-lam_ref[...])
    a = jnp.exp(log_a)
    bv = jnp.sqrt((1.0 - a) * (1.0 + a)) * (i_gate * xl)
    rowi = lax.broadcasted_iota(I32, (tb, 1), 0)
    d = 1
    while d < tb:
        keep = rowi >= d
        a_sh = jnp.where(keep, pltpu.roll(a, d, 0), 1.0)
        b_sh = jnp.where(keep, pltpu.roll(bv, d, 0), 0.0)
        bv = a * b_sh + bv
        a = a * a_sh
        d *= 2
    h = bv + a * h_ref[...]
    h_ref[...] = h[tb - 1:tb, :]
    o_ref[...] = h * _gelu_tanh(gate)


def _lru(p_lru, params, b, s):
    tb = min(512, s)
    full = lambda a: pl.BlockSpec(a.shape, lambda i, j: (0,) * a.ndim)
    return pl.pallas_call(
        _lru_kernel, grid=(b, s // tb),
        in_specs=[pl.BlockSpec((None, tb, LRU_SLAB), lambda i, j: (i, j, 0))] + [full(a) for a in params],
        out_specs=pl.BlockSpec((None, tb, LRU_W), lambda i, j: (i, j, 0)),
        out_shape=jax.ShapeDtypeStruct((b, s, LRU_W), F32),
        scratch_shapes=[pltpu.VMEM((tb + SUBLANES, LRU_W), F32), pltpu.VMEM((1, LRU_W), F32)],
        compiler_params=_cparams(2), name="rglru")(p_lru.reshape(b, s, LRU_SLAB), *params)


def _lru_params(conv_w, conv_b, wa, ba, wx, bx, lam):
    row = lambda v: v.reshape(1, -1)
    def bd(m):
        out = jnp.zeros((LRU_W, LRU_W), F32)
        for i in range(LRU_H):
            out = out.at[i * HEAD_DIM:(i + 1) * HEAD_DIM, i * HEAD_DIM:(i + 1) * HEAD_DIM].set(m[i])
        return out
    wax = jnp.concatenate([bd(wa), bd(wx)], axis=1)
    return (conv_w, row(conv_b), wax.astype(BF16), jnp.concatenate([row(ba), row(bx)], axis=1), row(lam))


ROW_TILE = D_MODEL // LANES


def _rows_from_tiles(ref, n, lead=(), first=0):
    return jnp.concatenate([ref[lead + (pl.ds(first * ROW_TILE + k, n, stride=ROW_TILE), slice(None))]
                            for k in range(ROW_TILE)], axis=-1)


def _rows_to_tiles(ref, x):
    for k in range(ROW_TILE):
        ref[pl.ds(k, x.shape[0], stride=ROW_TILE), :] = x[:, k * LANES:(k + 1) * LANES]


def _outproj_kernel(yr_ref, yw_ref, yl_ref, h_ref, w_ref, g_ref, b_ref, o_ref, ot_ref):
    ycat = jnp.concatenate([yr_ref[...], yw_ref[...], yl_ref[...]], axis=-1).astype(BF16)
    mix = jnp.dot(ycat, w_ref[...], preferred_element_type=F32)
    h1 = _layer_norm(ALPHA * h_ref[...] + mix, g_ref[...], b_ref[...])
    o_ref[...] = h1
    _rows_to_tiles(ot_ref, h1)


def _out_projection(y_ret, y_rwkv, y_lru, h, w_bf, g, b):
    t = h.shape[0]
    tm = min(512, t)
    row = lambda w: pl.BlockSpec((tm, w), lambda i: (i, 0))
    full = lambda a: pl.BlockSpec(a.shape, lambda i: (0,) * a.ndim)
    return pl.pallas_call(
        _outproj_kernel, grid=(t // tm,),
        in_specs=[row(RET_W), row(RWKV_W), row(LRU_W), row(D_MODEL), full(w_bf), full(g), full(b)],
        out_specs=[row(D_MODEL), pl.BlockSpec((tm * ROW_TILE, LANES), lambda i: (i, 0))],
        out_shape=[jax.ShapeDtypeStruct((t, D_MODEL), F32), jax.ShapeDtypeStruct((t * ROW_TILE, LANES), F32)],
        compiler_params=_cparams(1), name="out_proj_ln")(y_ret, y_rwkv, y_lru, h, w_bf, g, b)


def _router_kernel(h_ref, whi_ref, wlo_ref, b_ref, ew_ref, ei_ref, cnt_ref):
    @pl.when(pl.program_id(0) == 0)
    def _():
        cnt_ref[...] = jnp.zeros(cnt_ref.shape, F32)

    h = h_ref[...]
    h_hi = h.astype(BF16)
    h_lo = (h - h_hi.astype(F32)).astype(BF16)
    logits = (jnp.dot(h_hi, whi_ref[...], preferred_element_type=F32) + jnp.dot(h_lo, whi_ref[...], preferred_element_type=F32)
              + jnp.dot(h_hi, wlo_ref[...], preferred_element_type=F32)) + b_ref[...]
    lane = lax.broadcasted_iota(I32, logits.shape, 1)
    neg = -jnp.inf
    gmask = lane < N_GROUPS
    gl = jnp.where(gmask, logits, neg)
    gmax = jnp.max(gl, -1, keepdims=True)
    g_sel = jnp.min(jnp.where(gl == gmax, lane, LANES), -1, keepdims=True)
    g_gate = 1.0 / jnp.sum(jnp.where(gmask, jnp.exp(gl - gmax), 0.0), -1, keepdims=True)
    emask = (lane >= N_GROUPS) & (lane < N_GROUPS + N_EXPERTS) & (((lane - N_GROUPS) >> GROUP_SHIFT) == g_sel)
    el = jnp.where(emask, logits, neg)
    emax = jnp.max(el, -1, keepdims=True)
    ee = jnp.where(emask, jnp.exp(el - emax), 0.0)
    pe = ee / jnp.sum(ee, -1, keepdims=True)
    pe = jnp.where(emask, pe, -1.0)
    p1 = jnp.max(pe, -1, keepdims=True)
    i1 = jnp.min(jnp.where(pe == p1, lane, LANES), -1, keepdims=True)
    pe2 = jnp.where(lane == i1, -1.0, pe)
    p2 = jnp.max(pe2, -1, keepdims=True)
    i2 = jnp.min(jnp.where(pe2 == p2, lane, LANES), -1, keepdims=True)
    tot = p1 + p2
    ew_ref[...] = jnp.where(lane == 0, g_gate * (p1 / tot), jnp.where(lane == 1, g_gate * (p2 / tot), 0.0))
    tm = logits.shape[0]
    chosen = jnp.where((lane == i1) | (lane == i2), 1.0, 0.0)
    before = (lax.broadcasted_iota(I32, (tm, tm), 0) > lax.broadcasted_iota(I32, (tm, tm), 1))
    cum = jnp.dot(jnp.where(before, 1.0, 0.0).astype(BF16), chosen.astype(BF16), preferred_element_type=F32) + cnt_ref[0:1, :]
    r1 = jnp.sum(jnp.where(lane == i1, cum, 0.0), -1, keepdims=True).astype(I32)
    r2 = jnp.sum(jnp.where(lane == i2, cum, 0.0), -1, keepdims=True).astype(I32)
    cnt_ref[...] = jnp.broadcast_to(cnt_ref[0:1, :] + jnp.sum(chosen, 0, keepdims=True), cnt_ref.shape)
    ei_ref[...] = jnp.where(lane == 0, i1 - N_GROUPS, jnp.where(lane == 1, i2 - N_GROUPS,
                            jnp.where(lane == 2, r1, jnp.where(lane == 3, r2, 0))))


def _router(h, w_r, b_r):
    t = h.shape[0]
    tm = min(512, t)
    w_hi = w_r.astype(BF16)
    w_lo = (w_r - w_hi.astype(F32)).astype(BF16)
    row = lambda w: pl.BlockSpec((tm, w), lambda i: (i, 0))
    full = lambda a: pl.BlockSpec(a.shape, lambda i: (0,) * a.ndim)
    return pl.pallas_call(
        _router_kernel, grid=(t // tm,), in_specs=[row(D_MODEL), full(w_hi), full(w_lo), full(b_r)],
        out_specs=[row(LANES), row(LANES), pl.BlockSpec((SUBLANES, LANES), lambda i: (0, 0))],
        out_shape=[jax.ShapeDtypeStruct((t, LANES), F32), jax.ShapeDtypeStruct((t, LANES), I32),
                   jax.ShapeDtypeStruct((SUBLANES, LANES), F32)],
        compiler_params=_cparams(1), name="router")(h, w_hi, w_lo, b_r)


DMA_UNROLL = 8


def _tile_copy(src, s_row, dst, d_row, sem):
    return pltpu.make_async_copy(src.at[pl.ds(pl.multiple_of(s_row * ROW_TILE, ROW_TILE), ROW_TILE)],
                                 dst.at[pl.ds(pl.multiple_of(d_row * ROW_TILE, ROW_TILE), ROW_TILE)], sem)


def _expert_kernel(tok_ref, be_ref, nu_ref, ht_ref, wg_ref, wu_ref, wd_ref, o_ref, xbuf, sems, wg_s, wu_s, wd_s):
    i = pl.program_id(0)
    n_used = nu_ref[0]
    used = i < n_used
    new_expert = (i == 0) | (be_ref[i] != be_ref[jnp.maximum(i - 1, 0)])

    def row_copy(step, slot, r):
        return _tile_copy(ht_ref, tok_ref[step * MOE_BLOCK + r], xbuf.at[slot], r, sems.at[slot])

    def gather_loop(step, slot, start):
        def body(r2, c):
            for q in range(2):
                cp = row_copy(step, slot, 2 * r2 + q)
                if start:
                    cp.start(priority=q)
                else:
                    cp.wait()
            return c
        lax.fori_loop(0, MOE_BLOCK // 2, body, 0, unroll=DMA_UNROLL // 2)

    @pl.when(i == 0)
    def _():
        gather_loop(0, 0, True)

    @pl.when(i + 1 < n_used)
    def _():
        gather_loop(i + 1, (i + 1) & 1, True)

    @pl.when(used & new_expert)
    def _():
        wg_s[...] = wg_ref[...].astype(BF16)
        wu_s[...] = wu_ref[...].astype(BF16)
        wd_s[...] = wd_ref[...].astype(BF16)

    @pl.when(used)
    def _():
        slot = i & 1
        gather_loop(i, slot, False)
        x = _rows_from_tiles(xbuf, MOE_BLOCK, (slot,)).astype(BF16)
        hid = _silu(jnp.dot(x, wg_s[...], preferred_element_type=F32)) * jnp.dot(x, wu_s[...], preferred_element_type=F32)
        _rows_to_tiles(o_ref, jnp.dot(hid.astype(BF16), wd_s[...], preferred_element_type=F32))

    @pl.when(jnp.logical_not(used))
    def _():
        o_ref[...] = jnp.zeros(o_ref.shape, F32)


def _experts(h_tiles, slot_tok, block_exp, n_used, wg, wu, wd, layer):
    n_blocks = block_exp.shape[0]
    w_spec = lambda r, c: pl.BlockSpec((None, None, r, c), lambda i, tk, be, nu: (layer, be[i], 0, 0))
    return pl.pallas_call(
        _expert_kernel,
        grid_spec=pltpu.PrefetchScalarGridSpec(
            num_scalar_prefetch=3, grid=(n_blocks,),
            in_specs=[pl.BlockSpec(memory_space=pl.ANY),
                      w_spec(D_MODEL, D_EXPERT), w_spec(D_MODEL, D_EXPERT), w_spec(D_EXPERT, D_MODEL)],
            out_specs=pl.BlockSpec((MOE_BLOCK * ROW_TILE, LANES), lambda i, tk, be, nu: (i, 0)),
            scratch_shapes=[pltpu.VMEM((2, MOE_BLOCK * ROW_TILE, LANES), F32), pltpu.SemaphoreType.DMA((2,)),
                            pltpu.VMEM((D_MODEL, D_EXPERT), BF16), pltpu.VMEM((D_MODEL, D_EXPERT), BF16),
                            pltpu.VMEM((D_EXPERT, D_MODEL), BF16)]),
        out_shape=jax.ShapeDtypeStruct((n_blocks * MOE_BLOCK * ROW_TILE, LANES), F32),
        compiler_params=_cparams(1), name="moe_experts")(slot_tok, block_exp, n_used, h_tiles, wg, wu, wd)


def _combine_kernel(dest_ref, yb_ref, ew_ref, h_ref, g_ref, b_ref, o_ref, buf, sems):
    tc = h_ref.shape[0]
    i = pl.program_id(0)
    n = pl.num_programs(0)

    def copy_loop(step, slot, start):
        def body(r, c):
            for j in range(2):
                cp = _tile_copy(yb_ref, dest_ref[2 * (step * tc + r) + j], buf.at[slot, j], r, sems.at[slot])
                if start:
                    cp.start(priority=j)
                else:
                    cp.wait()
            return c
        lax.fori_loop(0, tc, body, 0, unroll=DMA_UNROLL)

    @pl.when(i == 0)
    def _():
        copy_loop(0, 0, True)

    @pl.when(i + 1 < n)
    def _():
        copy_loop(i + 1, (i + 1) & 1, True)

    slot = i & 1
    copy_loop(i, slot, False)
    ew = ew_ref[...]
    y = ew[:, 0:1] * _rows_from_tiles(buf, tc, (slot, 0)) + ew[:, 1:2] * _rows_from_tiles(buf, tc, (slot, 1))
    o_ref[...] = _layer_norm(ALPHA * h_ref[...] + y, g_ref[...], b_ref[...])


def _combine(yb_tiles, dest, ew, h, g, b):
    t = h.shape[0]
    tc = min(256, t)
    row = lambda w: pl.BlockSpec((tc, w), lambda i, d: (i, 0))
    full = lambda a: pl.BlockSpec(a.shape, lambda i, d: (0,) * a.ndim)
    return pl.pallas_call(
        _combine_kernel,
        grid_spec=pltpu.PrefetchScalarGridSpec(
            num_scalar_prefetch=1, grid=(t // tc,),
            in_specs=[pl.BlockSpec(memory_space=pl.ANY), row(LANES), row(D_MODEL), full(g), full(b)],
            out_specs=row(D_MODEL),
            scratch_shapes=[pltpu.VMEM((2, 2, tc * ROW_TILE, LANES), F32), pltpu.SemaphoreType.DMA((2,))]),
        out_shape=jax.ShapeDtypeStruct((t, D_MODEL), F32),
        compiler_params=_cparams(1), name="moe_combine_ln")(dest, yb_tiles, ew, h, g, b)


def _moe_plan(ei, cnt, t):
    a = t * 2
    counts = cnt[0, N_GROUPS:N_GROUPS + N_EXPERTS].astype(I32)
    padded = ((counts + MOE_BLOCK - 1) // MOE_BLOCK) * MOE_BLOCK
    pends = jnp.cumsum(padded)
    pstarts = pends - padded
    e_ids = jnp.arange(N_EXPERTS, dtype=I32)[None, None, :]
    start_of = jnp.sum(jnp.where(ei[:, 0:2, None] == e_ids, pstarts[None, None, :], 0), axis=-1)
    e_flat = ei[:, 0:2].reshape(a)
    dest = (start_of + ei[:, 2:4]).reshape(a).astype(I32)
    n_blocks = -(-a // MOE_BLOCK) + N_EXPERTS
    block_row0 = jnp.arange(n_blocks, dtype=I32) * MOE_BLOCK
    block_exp = jnp.minimum(jnp.sum((pends[None, :] <= block_row0[:, None]).astype(I32), axis=1), N_EXPERTS - 1)
    n_used = (pends[-1:] // MOE_BLOCK).astype(I32)
    tok_sorted = jnp.argsort(e_flat, stable=True).astype(I32) >> 1
    in_expert = block_row0 - pstarts[block_exp]
    src = (jnp.cumsum(counts) - counts)[block_exp] + in_expert
    lane_r = jnp.arange(MOE_BLOCK, dtype=I32)[None, :]
    valid = (in_expert[:, None] + lane_r) < counts[block_exp][:, None]
    slot_tok = jnp.where(valid, tok_sorted[jnp.clip(src[:, None] + lane_r, 0, a - 1)], 0).reshape(-1)
    return dest, slot_tok.astype(I32), block_exp.astype(I32), n_used


def _hier_moe_ln(h, h_tiles, w_r, b_r, wg, wu, wd, layer, g, b):
    t = h.shape[0]
    ew, ei, cnt = _router(h, w_r, b_r)
    dest, slot_tok, block_exp, n_used = _moe_plan(ei, cnt, t)
    yb_tiles = _experts(h_tiles, slot_tok, block_exp, n_used, wg, wu, wd, layer)
    return _combine(yb_tiles, dest, ew, h, g, b)


def kernel(x, ln_in_g, ln_in_b, w_in, w_out, rwkv_mu, rwkv_w0, rwkv_w2, rwkv_a0, rwkv_a2, rwkv_g2, rwkv_k_k, rwkv_k_a, rwkv_r_k, rwkv_gn_g, rwkv_gn_b, lru_conv_w, lru_conv_b, lru_wa, lru_ba, lru_wx, lru_bx, lru_lambda, ln1_g, ln1_b, moe_wg, moe_bg, moe_we, moe_be, moe_w_gate, moe_w_up, moe_w_down, ln2_g, ln2_b):
    b, s, d = x.shape
    t = b * s
    row = lambda v: v.reshape(1, -1)
    ret_consts = _retention_consts(s)
    h = x.reshape(t, d)
    for l in range(DEPTH):
        w_in_bf = w_in[l].astype(BF16)
        if l == 0:
            h, p_ret, p_rwkv, p_lru = _in_projection(h, w_in_bf, ln=(row(ln_in_g), row(ln_in_b)))
        else:
            p_ret, p_rwkv, p_lru = _in_projection(h, w_in_bf)
        y_ret = _retention(p_ret, ret_consts, b, s)
        y_rwkv = _rwkv(p_rwkv, _rwkv_params(rwkv_mu[l], rwkv_w0[l], rwkv_w2[l], rwkv_a0[l], rwkv_a2[l], rwkv_g2[l],
                                            rwkv_k_k[l], rwkv_k_a[l], rwkv_r_k[l], rwkv_gn_g[l], rwkv_gn_b[l]), b, s)
        y_lru = _lru(p_lru, _lru_params(lru_conv_w[l], lru_conv_b[l], lru_wa[l], lru_ba[l], lru_wx[l], lru_bx[l],
                                        lru_lambda[l]), b, s)
        h, h_tiles = _out_projection(y_ret.reshape(t, RET_W), y_rwkv.reshape(t, RWKV_W), y_lru.reshape(t, LRU_W), h,
                                     w_out[l].astype(BF16), row(ln1_g[l]), row(ln1_b[l]))
        pad = jnp.zeros((d, LANES - N_GROUPS - N_EXPERTS), F32)
        w_r = jnp.concatenate([moe_wg[l], moe_we[l], pad], axis=1)
        b_r = jnp.concatenate([moe_bg[l], moe_be[l], jnp.zeros((LANES - N_GROUPS - N_EXPERTS,), F32)]).reshape(1, LANES)
        h = _hier_moe_ln(h, h_tiles, w_r, b_r, moe_w_gate, moe_w_up, moe_w_down, l, row(ln2_g[l]), row(ln2_b[l]))
    return h.reshape(b, s, d)
```

```python
import jax
import jax.numpy as jnp
from jax import lax
from jax.experimental import pallas as pl
from jax.experimental.pallas import tpu as pltpu

F32 = jnp.float32
BF16 = jnp.bfloat16
I32 = jnp.int32

D_MODEL = 1024
DEPTH = 2
HEAD_DIM = 64
HEAD_SHIFT = 6
RET_H = 6
RET_W = RET_H * HEAD_DIM
RET_CHUNK = 128
ROPE_BASE = 10000.0
RWKV_H = 6
RWKV_W = RWKV_H * HEAD_DIM
RWKV_W_LORA = 64
RWKV_A_LORA = 64
RWKV_G_LORA = 128
RWKV_GN_EPS = 64e-5
LRU_H = 4
LRU_W = LRU_H * HEAD_DIM
LRU_CONV = 4
LRU_C = 8.0
RET_SLAB = 4 * RET_W
RWKV_SLAB = 3 * RWKV_W + RWKV_W_LORA + RWKV_A_LORA + RWKV_G_LORA
LRU_SLAB = 2 * LRU_W
IN_W = RET_SLAB + RWKV_SLAB + LRU_SLAB
N_GROUPS = 4
EXPERTS_PER_GROUP = 8
GROUP_SHIFT = 3
N_EXPERTS = N_GROUPS * EXPERTS_PER_GROUP
D_EXPERT = D_MODEL // 2
MOE_BLOCK = 256
LN_EPS = 1e-5
ALPHA = (2.0 * DEPTH) ** 0.25

LANES = 128
SUBLANES = 8
N_PAIRS = RET_W // LANES
RWKV_CHUNK = 64
RWKV_CHUNK_SHIFT = 6
VMEM_LIMIT = 56 * 1024 * 1024


def _cparams(n_axes, vmem=VMEM_LIMIT):
    return pltpu.CompilerParams(dimension_semantics=("arbitrary",) * n_axes, vmem_limit_bytes=vmem)


def _dot_bf(a, b):
    return jnp.dot(a.astype(BF16), b.astype(BF16), preferred_element_type=F32)


def _split_dot(a, b, passes, split_lhs):
    rem = a if split_lhs else b
    acc = None
    for _ in range(passes):
        piece = rem.astype(BF16)
        part = jnp.dot(piece, b, preferred_element_type=F32) if split_lhs else jnp.dot(a, piece, preferred_element_type=F32)
        acc = part if acc is None else acc + part
        rem = rem - piece.astype(F32)
    return acc


def _bmm(a, b):
    return jnp.einsum('nij,njk->nik', a.astype(BF16), b.astype(BF16), preferred_element_type=F32)


def _sigmoid(x):
    return 1.0 / (1.0 + jnp.exp(-x))


def _silu(x):
    return x * _sigmoid(x)


def _softplus(x):
    return jnp.maximum(x, 0.0) + jnp.log1p(jnp.exp(-jnp.abs(x)))


def _gelu_tanh(x):
    return 0.5 * x * (1.0 + jnp.tanh(0.7978845608028654 * (x + 0.044715 * (x * x * x))))


def _layer_norm(x, g, b):
    mu = jnp.mean(x, -1, keepdims=True)
    xc = x - mu
    var = jnp.mean(xc * xc, -1, keepdims=True)
    return xc * lax.rsqrt(var + LN_EPS) * g + b


def _pair_masks():
    lane = lax.broadcasted_iota(I32, (1, LANES), 1)
    m_a = lane < HEAD_DIM
    row = lax.broadcasted_iota(I32, (LANES, LANES), 0)
    col = lax.broadcasted_iota(I32, (LANES, LANES), 1)
    same_head = (row >> HEAD_SHIFT) == (col >> HEAD_SHIFT)
    return m_a, row, col, same_head


def _head_norm_pair(y, same_head, eps):
    avg = jnp.where(same_head, 1.0 / HEAD_DIM, 0.0).astype(BF16)
    mu = _split_dot(y, avg, 2, True)
    yc = y - mu
    var = _split_dot(yc * yc, avg, 2, True)
    return yc * lax.rsqrt(var + eps)


def _inproj_ln_kernel(x_ref, g_ref, b_ref, w_ref, h_ref, pr_ref, pw_ref, pl_ref):
    h = _layer_norm(x_ref[...], g_ref[...], b_ref[...])
    h_ref[...] = h
    p = jnp.dot(h.astype(BF16), w_ref[...], preferred_element_type=F32)
    pr_ref[...] = p[:, :RET_SLAB]
    pw_ref[...] = p[:, RET_SLAB:RET_SLAB + RWKV_SLAB]
    pl_ref[...] = p[:, RET_SLAB + RWKV_SLAB:]


def _inproj_kernel(h_ref, w_ref, pr_ref, pw_ref, pl_ref):
    p = jnp.dot(h_ref[...].astype(BF16), w_ref[...], preferred_element_type=F32)
    pr_ref[...] = p[:, :RET_SLAB]
    pw_ref[...] = p[:, RET_SLAB:RET_SLAB + RWKV_SLAB]
    pl_ref[...] = p[:, RET_SLAB + RWKV_SLAB:]


def _in_projection(x2d, w_bf, ln=None):
    t = x2d.shape[0]
    tm = min(512, t)
    row = lambda w: pl.BlockSpec((tm, w), lambda i: (i, 0))
    full = lambda a: pl.BlockSpec(a.shape, lambda i: (0,) * a.ndim)
    outs = [jax.ShapeDtypeStruct((t, RET_SLAB), F32), jax.ShapeDtypeStruct((t, RWKV_SLAB), F32),
            jax.ShapeDtypeStruct((t, LRU_SLAB), F32)]
    out_specs = [row(RET_SLAB), row(RWKV_SLAB), row(LRU_SLAB)]
    if ln is None:
        return pl.pallas_call(
            _inproj_kernel, grid=(t // tm,), in_specs=[row(D_MODEL), full(w_bf)], out_specs=out_specs,
            out_shape=outs, compiler_params=_cparams(1), name="in_proj")(x2d, w_bf)
    g, b = ln
    return pl.pallas_call(
        _inproj_ln_kernel, grid=(t // tm,), in_specs=[row(D_MODEL), full(g), full(b), full(w_bf)],
        out_specs=[row(D_MODEL)] + out_specs, out_shape=[jax.ShapeDtypeStruct((t, D_MODEL), F32)] + outs,
        compiler_params=_cparams(1), name="ln_in_proj")(x2d, g, b, w_bf)


def _swap_halves(t):
    n = t.shape[-1]
    lane = lax.broadcasted_iota(I32, (1, n), 1)
    lower = (lane & (HEAD_DIM - 1)) < (HEAD_DIM // 2)
    return jnp.where(lower, pltpu.roll(t, n - HEAD_DIM // 2, 1), pltpu.roll(t, HEAD_DIM // 2, 1))


def _retention_kernel(p_ref, cos_ref, sin_ref, dmat_ref, qw_ref, kw_ref, gch_ref, o_ref, st_ref):
    @pl.when(pl.program_id(1) == 0)
    def _():
        st_ref[...] = jnp.zeros(st_ref.shape, F32)

    tb = p_ref.shape[0]
    c = RET_CHUNK
    n_ch = tb // c
    pairs = [slice(pi * LANES, (pi + 1) * LANES) for pi in range(N_PAIRS)]
    m_a, _, _, same_head = _pair_masks()
    m_b = jnp.logical_not(m_a)
    cos = jnp.concatenate([cos_ref[...]] * N_PAIRS, axis=-1)
    sin = jnp.concatenate([sin_ref[...]] * N_PAIRS, axis=-1)
    q = p_ref[:, 0:RET_W]
    k = p_ref[:, RET_W:2 * RET_W]
    v = p_ref[:, 2 * RET_W:3 * RET_W]
    q = q * cos + _swap_halves(q) * sin
    k = (k * cos + _swap_halves(k) * sin) * (HEAD_DIM ** -0.5)

    def per_problem(fn):
        return jnp.stack([fn(slice(ci * c, (ci + 1) * c), pi) for ci in range(n_ch) for pi in range(N_PAIRS)])

    def stack(x):
        return jnp.concatenate([jnp.where(m_a, x, 0.0), jnp.where(m_b, x, 0.0)], axis=0)

    k_b = per_problem(lambda rows, pi: k[rows, pairs[pi]]).astype(BF16)
    v_b = per_problem(lambda rows, pi: v[rows, pairs[pi]]).astype(BF16)
    q_st = per_problem(lambda rows, pi: stack(q[rows, pairs[pi]])).astype(BF16)
    v_st = per_problem(lambda rows, pi: stack(v[rows, pairs[pi]])).astype(BF16)
    q_w = per_problem(lambda rows, pi: q[rows, pairs[pi]] * qw_ref[:, pairs[pi]]).astype(BF16)
    kw_t = jnp.swapaxes(per_problem(lambda rows, pi: k[rows, pairs[pi]] * kw_ref[:, pairs[pi]]), 1, 2).astype(BF16)
    decay = per_problem(lambda rows, pi: jnp.concatenate([dmat_ref[2 * pi], dmat_ref[2 * pi + 1]], axis=0))

    s = jnp.einsum('nik,njk->nij', q_st, k_b, preferred_element_type=F32) * decay
    intra = _bmm(jnp.concatenate([s[:, :c], s[:, c:]], axis=2), v_st)
    kv = _bmm(kw_t, v_b)
    states = []
    for ci in range(n_ch):
        for pi in range(N_PAIRS):
            st = st_ref[pi]
            states.append(st)
            st_ref[pi] = st * gch_ref[:, pairs[pi]] + jnp.where(same_head, kv[ci * N_PAIRS + pi], 0.0)
    y_all = intra + _bmm(q_w, jnp.stack(states))
    y = jnp.concatenate([jnp.concatenate([y_all[ci * N_PAIRS + pi] for pi in range(N_PAIRS)], axis=1)
                         for ci in range(n_ch)], axis=0)
    yn = jnp.concatenate([_head_norm_pair(y[:, ln], same_head, LN_EPS) for ln in pairs], axis=1)
    o_ref[...] = yn * _silu(p_ref[:, 3 * RET_W:4 * RET_W])


def _retention(p_ret, consts, b, s):
    tb = min(512, s)
    cos_t, sin_t, dmat, qw, kw, gch = consts
    full = lambda a: pl.BlockSpec(a.shape, lambda i, j: (0,) * a.ndim)
    return pl.pallas_call(
        _retention_kernel, grid=(b, s // tb),
        in_specs=[pl.BlockSpec((None, tb, RET_SLAB), lambda i, j: (i, j, 0)),
                  pl.BlockSpec((tb, LANES), lambda i, j: (j, 0)), pl.BlockSpec((tb, LANES), lambda i, j: (j, 0)),
                  full(dmat), full(qw), full(kw), full(gch)],
        out_specs=pl.BlockSpec((None, tb, RET_W), lambda i, j: (i, j, 0)),
        out_shape=jax.ShapeDtypeStruct((b, s, RET_W), F32),
        scratch_shapes=[pltpu.VMEM((N_PAIRS, LANES, LANES), F32)],
        compiler_params=_cparams(2), name="retention")(p_ret.reshape(b, s, RET_SLAB), cos_t, sin_t, dmat, qw, kw, gch)


def _retention_consts(s):
    half = HEAD_DIM // 2
    inv = 1.0 / (ROPE_BASE ** (jnp.arange(0, HEAD_DIM, 2, dtype=F32) / HEAD_DIM))
    ang = jnp.arange(s, dtype=F32)[:, None] * inv[None, :]
    cos, sin = jnp.cos(ang), jnp.sin(ang)
    cos_t = jnp.tile(cos, (1, LANES // half))
    sin_t = jnp.tile(jnp.concatenate([-sin, sin], axis=-1), (1, LANES // HEAD_DIM))
    c = RET_CHUNK
    log_g = jnp.log1p(-jnp.exp2(-5.0 - jnp.arange(RET_H, dtype=F32)))
    pos = jnp.arange(c, dtype=F32)
    diff = pos[:, None] - pos[None, :]
    causal = diff >= 0
    dmat = jnp.where(causal[None], jnp.exp(jnp.where(causal, diff, 0.0)[None] * log_g[:, None, None]), 0.0)
    lane_g = jnp.repeat(log_g, HEAD_DIM)[None, :]
    qw = jnp.exp((pos + 1.0)[:, None] * lane_g)
    kw = jnp.exp((c - 1.0 - pos)[:, None] * lane_g)
    gch = jnp.exp(c * lane_g)
    return cos_t, sin_t, dmat, qw, kw, gch


def _rwkv_kernel(p_ref, mu_ref, wwa_ref, w0_ref, a0_ref, g2_ref, kk_ref, ka_ref, rk_ref, gng_ref, gnb_ref,
                 o_ref, st_ref, carry_ref, rm_s, g_s, y0_s):
    tb = p_ref.shape[0]
    c = RWKV_CHUNK
    w = RWKV_W
    n_ch = tb // c
    pairs = [slice(pi * LANES, (pi + 1) * LANES) for pi in range(N_PAIRS)]

    @pl.when(pl.program_id(1) == 0)
    def _():
        st_ref[...] = jnp.zeros(st_ref.shape, F32)
        carry_ref[...] = jnp.zeros(carry_ref.shape, F32)

    m_a, row, col, same_head = _pair_masks()
    m_b = jnp.logical_not(m_a)
    head_ones = jnp.where(same_head, 1.0, 0.0).astype(BF16)

    def head_sums(x):
        return jnp.concatenate([_split_dot(x[:, ln], head_ones, 2, True) for ln in pairs], axis=1)

    p = p_ref[...]
    rowi = lax.broadcasted_iota(I32, (tb, 1), 0)
    prev = jnp.where(rowi == 0, carry_ref[...], pltpu.roll(p, 1, 0))
    carry_ref[...] = p[tb - 1:tb, :]
    z = p + (prev - p) * mu_ref[...]
    r = z[:, 0:w]
    kr = z[:, w:2 * w]
    vr = z[:, 2 * w:3 * w]
    wa = z[:, 3 * w:3 * w + LANES]
    gl = z[:, 3 * w + LANES:]
    lane = lax.broadcasted_iota(I32, (1, LANES), 1)
    pre = _dot_bf(jnp.where(lane < RWKV_W_LORA, jnp.tanh(wa), wa), wwa_ref[...])
    w_log = -_softplus(-(w0_ref[...] + pre[:, :w])) - 0.5
    a = _sigmoid(a0_ref[...] + pre[:, w:])
    g_rw = _dot_bf(_sigmoid(gl), g2_ref[...])
    kk = kr * kk_ref[...]
    kk = kk / jnp.maximum(jnp.sqrt(head_sums(kk * kk)), 1e-12)
    kmod = kr * (1.0 + (a - 1.0) * ka_ref[...])
    bonus = head_sums(r * kmod * rk_ref[...]) * vr
    ld = -jnp.exp(w_log)

    t_r = lax.broadcasted_iota(I32, (tb, tb), 0)
    t_c = lax.broadcasted_iota(I32, (tb, tb), 1)
    same_chunk = (t_r >> RWKV_CHUNK_SHIFT) == (t_c >> RWKV_CHUNK_SHIFT)
    lc = _split_dot(jnp.where(same_chunk & (t_r >= t_c), 1.0, 0.0).astype(BF16), ld, 3, False)
    lc_end = _split_dot(jnp.where(same_chunk, 1.0, 0.0).astype(BF16), ld, 3, False)
    p_inv = jnp.exp(-lc)
    p_rem = jnp.exp(lc_end - lc)
    p_end = jnp.exp(lc_end)
    b0 = kk * a

    def batch(x):
        return jnp.stack([jnp.concatenate([jnp.where(m_a, x[ci * c:(ci + 1) * c, ln], 0.0),
                                           jnp.where(m_b, x[ci * c:(ci + 1) * c, ln], 0.0)], axis=0)
                          for ci in range(n_ch) for ln in pairs])

    lr = batch(r * jnp.exp(lc))
    la_b = batch(-kk * jnp.exp(lc - ld)).astype(BF16)
    lr_b = lr.astype(BF16)
    rb_b = batch(b0 * p_inv).astype(BF16)
    rk_b = batch(kmod * p_inv).astype(BF16)
    vs_b = batch(vr).astype(BF16)
    be_t = jnp.swapaxes(batch(b0 * p_rem), 1, 2)
    ke_t = jnp.swapaxes(batch(kmod * p_rem), 1, 2)
    pe = jnp.stack([p_end[ci * c:ci * c + 1, ln] for ci in range(n_ch) for ln in pairs])

    tri_incl = (row >= col)[None]
    tri_strict = (row > col)[None]
    eye = (row == col)[None]
    xx = jnp.einsum('nik,njk->nij', jnp.concatenate([la_b, lr_b], axis=1), jnp.concatenate([rb_b, rk_b], axis=1),
                    preferred_element_type=F32)
    x_ab = jnp.where(tri_strict, xx[:, :LANES, :LANES], 0.0)
    x_ak = jnp.where(tri_strict, xx[:, :LANES, LANES:], 0.0)
    x_rb = jnp.where(tri_incl, xx[:, LANES:, :LANES], 0.0)
    x_rk = jnp.where(tri_incl, xx[:, LANES:, LANES:], 0.0)
    inv = None
    for sh in range(RWKV_CHUNK_SHIFT):
        lv = (((row >> (sh + 1)) == (col >> (sh + 1))) & (((row >> sh) & 1) == 1) & (((col >> sh) & 1) == 0))[None]
        x_lv = jnp.where(lv, x_ab, 0.0)
        inv = jnp.where(eye, 1.0, 0.0) + x_lv if inv is None else inv + _bmm(_bmm(inv, x_lv), inv)
    wy = _bmm(jnp.concatenate([x_ak, x_rk], axis=1), vs_b)
    aw_b = _bmm(inv, jnp.concatenate([la_b, wy[:, :LANES].astype(BF16)], axis=2)).astype(BF16)
    ba = _bmm(be_t, aw_b)
    m_mat = jnp.where(eye, pe, 0.0) + ba[:, :, :LANES]
    ra = _bmm(x_rb, aw_b)
    rm_s[...] = jnp.concatenate([lr + ra[:, :, :LANES], m_mat], axis=1).astype(BF16)
    g_s[...] = ba[:, :, LANES:] + _bmm(ke_t, vs_b)
    y0_s[...] = ra[:, :, LANES:] + wy[:, LANES:]

    st = [st_ref[pi] for pi in range(N_PAIRS)]
    y_rows = []
    for ci in range(n_ch):
        ys = []
        for pi in range(N_PAIRS):
            n = ci * N_PAIRS + pi
            sy = jnp.dot(rm_s[n], st[pi].astype(BF16), preferred_element_type=F32)
            yst = sy[:LANES] + y0_s[n]
            st[pi] = sy[LANES:] + g_s[n]
            ys.append(yst[:c] + yst[c:])
        y_rows.append(jnp.concatenate(ys, axis=1))
    for pi in range(N_PAIRS):
        st_ref[pi] = st[pi]
    y = jnp.concatenate(y_rows, axis=0)

    yn = jnp.concatenate([_head_norm_pair(y[:, ln], same_head, RWKV_GN_EPS) for ln in pairs], axis=1)
    o_ref[...] = (yn * gng_ref[...] + gnb_ref[...] + bonus) * g_rw


def _rwkv(p_rwkv, params, b, s):
    tb = min(256, s)
    n = (tb // RWKV_CHUNK) * N_PAIRS
    full = lambda a: pl.BlockSpec(a.shape, lambda i, j: (0,) * a.ndim)
    return pl.pallas_call(
        _rwkv_kernel, grid=(b, s // tb),
        in_specs=[pl.BlockSpec((None, tb, RWKV_SLAB), lambda i, j: (i, j, 0))] + [full(a) for a in params],
        out_specs=pl.BlockSpec((None, tb, RWKV_W), lambda i, j: (i, j, 0)),
        out_shape=jax.ShapeDtypeStruct((b, s, RWKV_W), F32),
        scratch_shapes=[pltpu.VMEM((N_PAIRS, LANES, LANES), F32), pltpu.VMEM((1, RWKV_SLAB), F32),
                        pltpu.VMEM((n, 2 * LANES, LANES), BF16), pltpu.VMEM((n, LANES, LANES), F32),
                        pltpu.VMEM((n, LANES, LANES), F32)],
        compiler_params=_cparams(2), name="rwkv7")(p_rwkv.reshape(b, s, RWKV_SLAB), *params)


def _rwkv_params(mu, w0, w2, a0, a2, g2, k_k, k_a, r_k, gn_g, gn_b):
    row = lambda v: v.reshape(1, -1)
    z = jnp.zeros((RWKV_W_LORA, RWKV_W), F32)
    wwa = jnp.concatenate([jnp.concatenate([w2, z], axis=1), jnp.concatenate([z, a2], axis=1)], axis=0)
    return (row(mu), wwa.astype(BF16), row(w0), row(a0), g2.astype(BF16), row(k_k), row(k_a), row(r_k), row(gn_g), row(gn_b))


def _lru_kernel(p_ref, cw_ref, cb_ref, wax_ref, bax_ref, lam_ref, o_ref, xs_ref, h_ref):
    tb = p_ref.shape[0]
    w = LRU_W

    @pl.when(pl.program_id(1) == 0)
    def _():
        xs_ref[0:SUBLANES, :] = jnp.zeros((SUBLANES, w), F32)
        h_ref[...] = jnp.zeros(h_ref.shape, F32)

    x = p_ref[:, 0:w]
    gate = p_ref[:, w:]
    xs_ref[SUBLANES:, :] = x
    xl = x * cw_ref[LRU_CONV - 1:LRU_CONV, :] + cb_ref[...]
    for d in range(1, LRU_CONV):
        xl = xl + xs_ref[pl.ds(SUBLANES - d, tb), :] * cw_ref[LRU_CONV - 1 - d:LRU_CONV - d, :]
    xs_ref[0:SUBLANES, :] = x[tb - SUBLANES:, :]
    gates = _dot_bf(xl, wax_ref[...]) + bax_ref[...]
    r_gate = _sigmoid(gates[:, :w])
    i_gate = _sigmoid(gates[:, w:])
    log_a = -LRU_C * r_gate * _softplus(-lam_ref[...])
    a = jnp.exp(log_a)
    bv = jnp.sqrt((1.0 - a) * (1.0 + a)) * (i_gate * xl)
    rowi = lax.broadcasted_iota(I32, (tb, 1), 0)
    d = 1
    while d < tb:
        keep = rowi >= d
        a_sh = jnp.where(keep, pltpu.roll(a, d, 0), 1.0)
        b_sh = jnp.where(keep, pltpu.roll(bv, d, 0), 0.0)
        bv = a * b_sh + bv
        a = a * a_sh
        d *= 2
    h = bv + a * h_ref[...]
    h_ref[...] = h[tb - 1:tb, :]
    o_ref[...] = h * _gelu_tanh(gate)


def _lru(p_lru, params, b, s):
    tb = min(512, s)
    full = lambda a: pl.BlockSpec(a.shape, lambda i, j: (0,) * a.ndim)
    return pl.pallas_call(
        _lru_kernel, grid=(b, s // tb),
        in_specs=[pl.BlockSpec((None, tb, LRU_SLAB), lambda i, j: (i, j, 0))] + [full(a) for a in params],
        out_specs=pl.BlockSpec((None, tb, LRU_W), lambda i, j: (i, j, 0)),
        out_shape=jax.ShapeDtypeStruct((b, s, LRU_W), F32),
        scratch_shapes=[pltpu.VMEM((tb + SUBLANES, LRU_W), F32), pltpu.VMEM((1, LRU_W), F32)],
        compiler_params=_cparams(2), name="rglru")(p_lru.reshape(b, s, LRU_SLAB), *params)


def _lru_params(conv_w, conv_b, wa, ba, wx, bx, lam):
    row = lambda v: v.reshape(1, -1)
    def bd(m):
        out = jnp.zeros((LRU_W, LRU_W), F32)
        for i in range(LRU_H):
            out = out.at[i * HEAD_DIM:(i + 1) * HEAD_DIM, i * HEAD_DIM:(i + 1) * HEAD_DIM].set(m[i])
        return out
    wax = jnp.concatenate([bd(wa), bd(wx)], axis=1)
    return (conv_w, row(conv_b), wax.astype(BF16), jnp.concatenate([row(ba), row(bx)], axis=1), row(lam))


ROW_TILE = D_MODEL // LANES


def _rows_from_tiles(ref, n, lead=(), first=0):
    return jnp.concatenate([ref[lead + (pl.ds(first * ROW_TILE + k, n, stride=ROW_TILE), slice(None))]
                            for k in range(ROW_TILE)], axis=-1)


def _rows_to_tiles(ref, x):
    for k in range(ROW_TILE):
        ref[pl.ds(k, x.shape[0], stride=ROW_TILE), :] = x[:, k * LANES:(k + 1) * LANES]


def _outproj_kernel(yr_ref, yw_ref, yl_ref, h_ref, w_ref, g_ref, b_ref, o_ref, ot_ref):
    ycat = jnp.concatenate([yr_ref[...], yw_ref[...], yl_ref[...]], axis=-1).astype(BF16)
    mix = jnp.dot(ycat, w_ref[...], preferred_element_type=F32)
    h1 = _layer_norm(ALPHA * h_ref[...] + mix, g_ref[...], b_ref[...])
    o_ref[...] = h1
    _rows_to_tiles(ot_ref, h1)


def _out_projection(y_ret, y_rwkv, y_lru, h, w_bf, g, b):
    t = h.shape[0]
    tm = min(512, t)
    row = lambda w: pl.BlockSpec((tm, w), lambda i: (i, 0))
    full = lambda a: pl.BlockSpec(a.shape, lambda i: (0,) * a.ndim)
    return pl.pallas_call(
        _outproj_kernel, grid=(t // tm,),
        in_specs=[row(RET_W), row(RWKV_W), row(LRU_W), row(D_MODEL), full(w_bf), full(g), full(b)],
        out_specs=[row(D_MODEL), pl.BlockSpec((tm * ROW_TILE, LANES), lambda i: (i, 0))],
        out_shape=[jax.ShapeDtypeStruct((t, D_MODEL), F32), jax.ShapeDtypeStruct((t * ROW_TILE, LANES), F32)],
        compiler_params=_cparams(1), name="out_proj_ln")(y_ret, y_rwkv, y_lru, h, w_bf, g, b)


def _router_kernel(h_ref, whi_ref, wlo_ref, b_ref, ew_ref, ei_ref, cnt_ref):
    @pl.when(pl.program_id(0) == 0)
    def _():
        cnt_ref[...] = jnp.zeros(cnt_ref.shape, F32)

    h = h_ref[...]
    h_hi = h.astype(BF16)
    h_lo = (h - h_hi.astype(F32)).astype(BF16)
    logits = (jnp.dot(h_hi, whi_ref[...], preferred_element_type=F32) + jnp.dot(h_lo, whi_ref[...], preferred_element_type=F32)
              + jnp.dot(h_hi, wlo_ref[...], preferred_element_type=F32)) + b_ref[...]
    lane = lax.broadcasted_iota(I32, logits.shape, 1)
    neg = -jnp.inf
    gmask = lane < N_GROUPS
    gl = jnp.where(gmask, logits, neg)
    gmax = jnp.max(gl, -1, keepdims=True)
    g_sel = jnp.min(jnp.where(gl == gmax, lane, LANES), -1, keepdims=True)
    g_gate = 1.0 / jnp.sum(jnp.where(gmask, jnp.exp(gl - gmax), 0.0), -1, keepdims=True)
    emask = (lane >= N_GROUPS) & (lane < N_GROUPS + N_EXPERTS) & (((lane - N_GROUPS) >> GROUP_SHIFT) == g_sel)
    el = jnp.where(emask, logits, neg)
    emax = jnp.max(el, -1, keepdims=True)
    ee = jnp.where(emask, jnp.exp(el - emax), 0.0)
    pe = ee / jnp.sum(ee, -1, keepdims=True)
    pe = jnp.where(emask, pe, -1.0)
    p1 = jnp.max(pe, -1, keepdims=True)
    i1 = jnp.min(jnp.where(pe == p1, lane, LANES), -1, keepdims=True)
    pe2 = jnp.where(lane == i1, -1.0, pe)
    p2 = jnp.max(pe2, -1, keepdims=True)
    i2 = jnp.min(jnp.where(pe2 == p2, lane, LANES), -1, keepdims=True)
    tot = p1 + p2
    ew_ref[...] = jnp.where(lane == 0, g_gate * (p1 / tot), jnp.where(lane == 1, g_gate * (p2 / tot), 0.0))
    tm = logits.shape[0]
    chosen = jnp.where((lane == i1) | (lane == i2), 1.0, 0.0)
    before = (lax.broadcasted_iota(I32, (tm, tm), 0) > lax.broadcasted_iota(I32, (tm, tm), 1))
    cum = jnp.dot(jnp.where(before, 1.0, 0.0).astype(BF16), chosen.astype(BF16), preferred_element_type=F32) + cnt_ref[0:1, :]
    r1 = jnp.sum(jnp.where(lane == i1, cum, 0.0), -1, keepdims=True).astype(I32)
    r2 = jnp.sum(jnp.where(lane == i2, cum, 0.0), -1, keepdims=True).astype(I32)
    cnt_ref[...] = jnp.broadcast_to(cnt_ref[0:1, :] + jnp.sum(chosen, 0, keepdims=True), cnt_ref.shape)
    ei_ref[...] = jnp.where(lane == 0, i1 - N_GROUPS, jnp.where(lane == 1, i2 - N_GROUPS,
                            jnp.where(lane == 2, r1, jnp.where(lane == 3, r2, 0))))


def _router(h, w_r, b_r):
    t = h.shape[0]
    tm = min(512, t)
    w_hi = w_r.astype(BF16)
    w_lo = (w_r - w_hi.astype(F32)).astype(BF16)
    row = lambda w: pl.BlockSpec((tm, w), lambda i: (i, 0))
    full = lambda a: pl.BlockSpec(a.shape, lambda i: (0,) * a.ndim)
    return pl.pallas_call(
        _router_kernel, grid=(t // tm,), in_specs=[row(D_MODEL), full(w_hi), full(w_lo), full(b_r)],
        out_specs=[row(LANES), row(LANES), pl.BlockSpec((SUBLANES, LANES), lambda i: (0, 0))],
        out_shape=[jax.ShapeDtypeStruct((t, LANES), F32), jax.ShapeDtypeStruct((t, LANES), I32),
                   jax.ShapeDtypeStruct((SUBLANES, LANES), F32)],
        compiler_params=_cparams(1), name="router")(h, w_hi, w_lo, b_r)


DMA_UNROLL = 8
MXU_TILE = 256


def _tile_copy(src, s_row, dst, d_row, sem):
    return pltpu.make_async_copy(src.at[pl.ds(pl.multiple_of(s_row * ROW_TILE, ROW_TILE), ROW_TILE)],
                                 dst.at[pl.ds(pl.multiple_of(d_row * ROW_TILE, ROW_TILE), ROW_TILE)], sem)


def _expert_kernel(tok_ref, be_ref, nu_ref, ht_ref, wg_ref, wu_ref, wd_ref, o_ref, xbuf, sems, wg_s, wu_s, wd_s):
    i = pl.program_id(0)
    n_used = nu_ref[0]
    used = i < n_used
    new_expert = (i == 0) | (be_ref[i] != be_ref[jnp.maximum(i - 1, 0)])

    def row_copy(step, slot, r):
        return _tile_copy(ht_ref, tok_ref[step * MOE_BLOCK + r], xbuf.at[slot], r, sems.at[slot])

    def gather_loop(step, slot, start):
        def body(r2, c):
            for q in range(2):
                cp = row_copy(step, slot, 2 * r2 + q)
                if start:
                    cp.start(priority=q)
                else:
                    cp.wait()
            return c
        lax.fori_loop(0, MOE_BLOCK // 2, body, 0, unroll=DMA_UNROLL // 2)

    @pl.when(i == 0)
    def _():
        gather_loop(0, 0, True)

    @pl.when(used & new_expert)
    def _():
        wg_s[...] = wg_ref[...].astype(BF16)
        wu_s[...] = wu_ref[...].astype(BF16)
        wd_s[...] = wd_ref[...].astype(BF16)

    @pl.when(used)
    def _():
        slot = i & 1
        gather_loop(i, slot, False)
        x = _rows_from_tiles(xbuf, MOE_BLOCK, (slot,)).astype(BF16)
        n_dots = 2 * (D_MODEL // MXU_TILE) * (D_EXPERT // MXU_TILE) + (D_EXPERT // MXU_TILE) * (D_MODEL // MXU_TILE)
        per_dot = -(-MOE_BLOCK // n_dots)
        issued = [0]

        def issue_some():
            for r in range(issued[0], min(issued[0] + per_dot, MOE_BLOCK)):
                row_copy(i + 1, 1 - slot, r).start()
            issued[0] = min(issued[0] + per_dot, MOE_BLOCK)

        def tiled_dot(a, w_ref, col):
            acc = None
            for kc in range(a.shape[1] // MXU_TILE):
                ks = slice(kc * MXU_TILE, (kc + 1) * MXU_TILE)
                part = jnp.dot(a[:, ks], w_ref[ks, col * MXU_TILE:(col + 1) * MXU_TILE], preferred_element_type=F32)
                acc = part if acc is None else acc + part
                issue_some()
            return acc

        hid = jnp.concatenate([_silu(tiled_dot(x, wg_s, c)) * tiled_dot(x, wu_s, c) for c in range(D_EXPERT // MXU_TILE)],
                              axis=1).astype(BF16)
        y = jnp.concatenate([tiled_dot(hid, wd_s, c) for c in range(D_MODEL // MXU_TILE)], axis=1)
        _rows_to_tiles(o_ref, y)

    @pl.when(i == n_used)
    def _():
        gather_loop(i, i & 1, False)

    @pl.when(jnp.logical_not(used))
    def _():
        o_ref[...] = jnp.zeros(o_ref.shape, F32)


def _experts(h_tiles, slot_tok, block_exp, n_used, wg, wu, wd, layer):
    n_blocks = block_exp.shape[0]
    w_spec = lambda r, c: pl.BlockSpec((None, None, r, c), lambda i, tk, be, nu: (layer, be[i], 0, 0))
    return pl.pallas_call(
        _expert_kernel,
        grid_spec=pltpu.PrefetchScalarGridSpec(
            num_scalar_prefetch=3, grid=(n_blocks,),
            in_specs=[pl.BlockSpec(memory_space=pl.ANY),
                      w_spec(D_MODEL, D_EXPERT), w_spec(D_MODEL, D_EXPERT), w_spec(D_EXPERT, D_MODEL)],
            out_specs=pl.BlockSpec((MOE_BLOCK * ROW_TILE, LANES), lambda i, tk, be, nu: (i, 0)),
            scratch_shapes=[pltpu.VMEM((2, MOE_BLOCK * ROW_TILE, LANES), F32), pltpu.SemaphoreType.DMA((2,)),
                            pltpu.VMEM((D_MODEL, D_EXPERT), BF16), pltpu.VMEM((D_MODEL, D_EXPERT), BF16),
                            pltpu.VMEM((D_EXPERT, D_MODEL), BF16)]),
        out_shape=jax.ShapeDtypeStruct((n_blocks * MOE_BLOCK * ROW_TILE, LANES), F32),
        compiler_params=_cparams(1), name="moe_experts")(slot_tok, block_exp, n_used, h_tiles, wg, wu, wd)


def _combine_kernel(dest_ref, yb_ref, ew_ref, h_ref, g_ref, b_ref, o_ref, buf, sems):
    tc = h_ref.shape[0]
    i = pl.program_id(0)
    n = pl.num_programs(0)

    def copy_loop(step, slot, start):
        def body(r, c):
            for j in range(2):
                cp = _tile_copy(yb_ref, dest_ref[2 * (step * tc + r) + j], buf.at[slot, j], r, sems.at[slot])
                if start:
                    cp.start(priority=j)
                else:
                    cp.wait()
            return c
        lax.fori_loop(0, tc, body, 0, unroll=DMA_UNROLL)

    @pl.when(i == 0)
    def _():
        copy_loop(0, 0, True)

    @pl.when(i + 1 < n)
    def _():
        copy_loop(i + 1, (i + 1) & 1, True)

    slot = i & 1
    copy_loop(i, slot, False)
    ew = ew_ref[...]
    y = ew[:, 0:1] * _rows_from_tiles(buf, tc, (slot, 0)) + ew[:, 1:2] * _rows_from_tiles(buf, tc, (slot, 1))
    o_ref[...] = _layer_norm(ALPHA * h_ref[...] + y, g_ref[...], b_ref[...])


def _combine(yb_tiles, dest, ew, h, g, b):
    t = h.shape[0]
    tc = min(256, t)
    row = lambda w: pl.BlockSpec((tc, w), lambda i, d: (i, 0))
    full = lambda a: pl.BlockSpec(a.shape, lambda i, d: (0,) * a.ndim)
    return pl.pallas_call(
        _combine_kernel,
        grid_spec=pltpu.PrefetchScalarGridSpec(
            num_scalar_prefetch=1, grid=(t // tc,),
            in_specs=[pl.BlockSpec(memory_space=pl.ANY), row(LANES), row(D_MODEL), full(g), full(b)],
            out_specs=row(D_MODEL),
            scratch_shapes=[pltpu.VMEM((2, 2, tc * ROW_TILE, LANES), F32), pltpu.SemaphoreType.DMA((2,))]),
        out_shape=jax.ShapeDtypeStruct((t, D_MODEL), F32),
        compiler_params=_cparams(1), name="moe_combine_ln")(dest, yb_tiles, ew, h, g, b)


def _moe_plan(ei, cnt, t):
    a = t * 2
    counts = cnt[0, N_GROUPS:N_GROUPS + N_EXPERTS].astype(I32)
    padded = ((counts + MOE_BLOCK - 1) // MOE_BLOCK) * MOE_BLOCK
    pends = jnp.cumsum(padded)
    pstarts = pends - padded
    e_ids = jnp.arange(N_EXPERTS, dtype=I32)[None, None, :]
    start_of = jnp.sum(jnp.where(ei[:, 0:2, None] == e_ids, pstarts[None, None, :], 0), axis=-1)
    e_flat = ei[:, 0:2].reshape(a)
    dest = (start_of + ei[:, 2:4]).reshape(a).astype(I32)
    n_blocks = -(-a // MOE_BLOCK) + N_EXPERTS
    block_row0 = jnp.arange(n_blocks, dtype=I32) * MOE_BLOCK
    block_exp = jnp.minimum(jnp.sum((pends[None, :] <= block_row0[:, None]).astype(I32), axis=1), N_EXPERTS - 1)
    n_used = (pends[-1:] // MOE_BLOCK).astype(I32)
    tok_sorted = jnp.argsort(e_flat, stable=True).astype(I32) >> 1
    in_expert = block_row0 - pstarts[block_exp]
    src = (jnp.cumsum(counts) - counts)[block_exp] + in_expert
    lane_r = jnp.arange(MOE_BLOCK, dtype=I32)[None, :]
    valid = (in_expert[:, None] + lane_r) < counts[block_exp][:, None]
    slot_tok = jnp.where(valid, tok_sorted[jnp.clip(src[:, None] + lane_r, 0, a - 1)], 0).reshape(-1)
    return dest, slot_tok.astype(I32), block_exp.astype(I32), n_used


def _hier_moe_ln(h, h_tiles, w_r, b_r, wg, wu, wd, layer, g, b):
    t = h.shape[0]
    ew, ei, cnt = _router(h, w_r, b_r)
    dest, slot_tok, block_exp, n_used = _moe_plan(ei, cnt, t)
    yb_tiles = _experts(h_tiles, slot_tok, block_exp, n_used, wg, wu, wd, layer)
    return _combine(yb_tiles, dest, ew, h, g, b)


def kernel(x, ln_in_g, ln_in_b, w_in, w_out, rwkv_mu, rwkv_w0, rwkv_w2, rwkv_a0, rwkv_a2, rwkv_g2, rwkv_k_k, rwkv_k_a, rwkv_r_k, rwkv_gn_g, rwkv_gn_b, lru_conv_w, lru_conv_b, lru_wa, lru_ba, lru_wx, lru_bx, lru_lambda, ln1_g, ln1_b, moe_wg, moe_bg, moe_we, moe_be, moe_w_gate, moe_w_up, moe_w_down, ln2_g, ln2_b):
    b, s, d = x.shape
    t = b * s
    row = lambda v: v.reshape(1, -1)
    ret_consts = _retention_consts(s)
    h = x.reshape(t, d)
    for l in range(DEPTH):
        w_in_bf = w_in[l].astype(BF16)
        if l == 0:
            h, p_ret, p_rwkv, p_lru = _in_projection(h, w_in_bf, ln=(row(ln_in_g), row(ln_in_b)))
        else:
            p_ret, p_rwkv, p_lru = _in_projection(h, w_in_bf)
        y_ret = _retention(p_ret, ret_consts, b, s)
        y_rwkv = _rwkv(p_rwkv, _rwkv_params(rwkv_mu[l], rwkv_w0[l], rwkv_w2[l], rwkv_a0[l], rwkv_a2[l], rwkv_g2[l],
                                            rwkv_k_k[l], rwkv_k_a[l], rwkv_r_k[l], rwkv_gn_g[l], rwkv_gn_b[l]), b, s)
        y_lru = _lru(p_lru, _lru_params(lru_conv_w[l], lru_conv_b[l], lru_wa[l], lru_ba[l], lru_wx[l], lru_bx[l],
                                        lru_lambda[l]), b, s)
        h, h_tiles = _out_projection(y_ret.reshape(t, RET_W), y_rwkv.reshape(t, RWKV_W), y_lru.reshape(t, LRU_W), h,
                                     w_out[l].astype(BF16), row(ln1_g[l]), row(ln1_b[l]))
        pad = jnp.zeros((d, LANES - N_GROUPS - N_EXPERTS), F32)
        w_r = jnp.concatenate([moe_wg[l], moe_we[l], pad], axis=1)
        b_r = jnp.concatenate([moe_bg[l], moe_be[l], jnp.zeros((LANES - N_GROUPS - N_EXPERTS,), F32)]).reshape(1, LANES)
        h = _hier_moe_ln(h, h_tiles, w_r, b_r, moe_w_gate, moe_w_up, moe_w_down, l, row(ln2_g[l]), row(ln2_b[l]))
    return h.reshape(b, s, d)
```

```python
import jax
import jax.numpy as jnp
from jax import lax
from jax.experimental import pallas as pl
from jax.experimental.pallas import tpu as pltpu

F32 = jnp.float32
BF16 = jnp.bfloat16
I32 = jnp.int32

D_MODEL = 1024
DEPTH = 2
HEAD_DIM = 64
HEAD_SHIFT = 6
RET_H = 6
RET_W = RET_H * HEAD_DIM
RET_CHUNK = 128
ROPE_BASE = 10000.0
RWKV_H = 6
RWKV_W = RWKV_H * HEAD_DIM
RWKV_W_LORA = 64
RWKV_A_LORA = 64
RWKV_G_LORA = 128
RWKV_GN_EPS = 64e-5
LRU_H = 4
LRU_W = LRU_H * HEAD_DIM
LRU_CONV = 4
LRU_C = 8.0
RET_SLAB = 4 * RET_W
RWKV_SLAB = 3 * RWKV_W + RWKV_W_LORA + RWKV_A_LORA + RWKV_G_LORA
LRU_SLAB = 2 * LRU_W
IN_W = RET_SLAB + RWKV_SLAB + LRU_SLAB
N_GROUPS = 4
EXPERTS_PER_GROUP = 8
GROUP_SHIFT = 3
N_EXPERTS = N_GROUPS * EXPERTS_PER_GROUP
D_EXPERT = D_MODEL // 2
MOE_BLOCK = 256
LN_EPS = 1e-5
ALPHA = (2.0 * DEPTH) ** 0.25

LANES = 128
SUBLANES = 8
N_PAIRS = RET_W // LANES
RWKV_CHUNK = 64
RWKV_CHUNK_SHIFT = 6
VMEM_LIMIT = 56 * 1024 * 1024


def _cparams(n_axes, vmem=VMEM_LIMIT):
    return pltpu.CompilerParams(dimension_semantics=("arbitrary",) * n_axes, vmem_limit_bytes=vmem)


def _dot_bf(a, b):
    return jnp.dot(a.astype(BF16), b.astype(BF16), preferred_element_type=F32)


def _split_dot(a, b, passes, split_lhs):
    rem = a if split_lhs else b
    acc = None
    for _ in range(passes):
        piece = rem.astype(BF16)
        part = jnp.dot(piece, b, preferred_element_type=F32) if split_lhs else jnp.dot(a, piece, preferred_element_type=F32)
        acc = part if acc is None else acc + part
        rem = rem - piece.astype(F32)
    return acc


def _bmm(a, b):
    return jnp.einsum('nij,njk->nik', a.astype(BF16), b.astype(BF16), preferred_element_type=F32)


def _sigmoid(x):
    return 1.0 / (1.0 + jnp.exp(-x))


def _silu(x):
    return x * _sigmoid(x)


def _softplus(x):
    return jnp.maximum(x, 0.0) + jnp.log1p(jnp.exp(-jnp.abs(x)))


def _gelu_tanh(x):
    return 0.5 * x * (1.0 + jnp.tanh(0.7978845608028654 * (x + 0.044715 * (x * x * x))))


def _layer_norm(x, g, b):
    mu = jnp.mean(x, -1, keepdims=True)
    xc = x - mu
    var = jnp.mean(xc * xc, -1, keepdims=True)
    return xc * lax.rsqrt(var + LN_EPS) * g + b


def _pair_masks():
    lane = lax.broadcasted_iota(I32, (1, LANES), 1)
    m_a = lane < HEAD_DIM
    row = lax.broadcasted_iota(I32, (LANES, LANES), 0)
    col = lax.broadcasted_iota(I32, (LANES, LANES), 1)
    same_head = (row >> HEAD_SHIFT) == (col >> HEAD_SHIFT)
    return m_a, row, col, same_head


def _head_norm_pair(y, same_head, eps):
    avg = jnp.where(same_head, 1.0 / HEAD_DIM, 0.0).astype(BF16)
    mu = _split_dot(y, avg, 2, True)
    yc = y - mu
    var = _split_dot(yc * yc, avg, 2, True)
    return yc * lax.rsqrt(var + eps)


def _inproj_ln_kernel(x_ref, g_ref, b_ref, w_ref, h_ref, pr_ref, pw_ref, pl_ref):
    h = _layer_norm(x_ref[...], g_ref[...], b_ref[...])
    h_ref[...] = h
    p = jnp.dot(h.astype(BF16), w_ref[...], preferred_element_type=F32)
    pr_ref[...] = p[:, :RET_SLAB]
    pw_ref[...] = p[:, RET_SLAB:RET_SLAB + RWKV_SLAB]
    pl_ref[...] = p[:, RET_SLAB + RWKV_SLAB:]


def _inproj_kernel(h_ref, w_ref, pr_ref, pw_ref, pl_ref):
    p = jnp.dot(h_ref[...].astype(BF16), w_ref[...], preferred_element_type=F32)
    pr_ref[...] = p[:, :RET_SLAB]
    pw_ref[...] = p[:, RET_SLAB:RET_SLAB + RWKV_SLAB]
    pl_ref[...] = p[:, RET_SLAB + RWKV_SLAB:]


def _in_projection(x2d, w_bf, ln=None):
    t = x2d.shape[0]
    tm = min(512, t)
    row = lambda w: pl.BlockSpec((tm, w), lambda i: (i, 0))
    full = lambda a: pl.BlockSpec(a.shape, lambda i: (0,) * a.ndim)
    outs = [jax.ShapeDtypeStruct((t, RET_SLAB), F32), jax.ShapeDtypeStruct((t, RWKV_SLAB), F32),
            jax.ShapeDtypeStruct((t, LRU_SLAB), F32)]
    out_specs = [row(RET_SLAB), row(RWKV_SLAB), row(LRU_SLAB)]
    if ln is None:
        return pl.pallas_call(
            _inproj_kernel, grid=(t // tm,), in_specs=[row(D_MODEL), full(w_bf)], out_specs=out_specs,
            out_shape=outs, compiler_params=_cparams(1), name="in_proj")(x2d, w_bf)
    g, b = ln
    return pl.pallas_call(
        _inproj_ln_kernel, grid=(t // tm,), in_specs=[row(D_MODEL), full(g), full(b), full(w_bf)],
        out_specs=[row(D_MODEL)] + out_specs, out_shape=[jax.ShapeDtypeStruct((t, D_MODEL), F32)] + outs,
        compiler_params=_cparams(1), name="ln_in_proj")(x2d, g, b, w_bf)


def _swap_halves(t):
    n = t.shape[-1]
    lane = lax.broadcasted_iota(I32, (1, n), 1)
    lower = (lane & (HEAD_DIM - 1)) < (HEAD_DIM // 2)
    return jnp.where(lower, pltpu.roll(t, n - HEAD_DIM // 2, 1), pltpu.roll(t, HEAD_DIM // 2, 1))


def _retention_kernel(p_ref, cos_ref, sin_ref, dmat_ref, qw_ref, kw_ref, gch_ref, o_ref, st_ref):
    @pl.when(pl.program_id(1) == 0)
    def _():
        st_ref[...] = jnp.zeros(st_ref.shape, F32)

    tb = p_ref.shape[0]
    c = RET_CHUNK
    n_ch = tb // c
    pairs = [slice(pi * LANES, (pi + 1) * LANES) for pi in range(N_PAIRS)]
    m_a, _, _, same_head = _pair_masks()
    m_b = jnp.logical_not(m_a)
    cos = jnp.concatenate([cos_ref[...]] * N_PAIRS, axis=-1)
    sin = jnp.concatenate([sin_ref[...]] * N_PAIRS, axis=-1)
    q = p_ref[:, 0:RET_W]
    k = p_ref[:, RET_W:2 * RET_W]
    v = p_ref[:, 2 * RET_W:3 * RET_W]
    q = q * cos + _swap_halves(q) * sin
    k = (k * cos + _swap_halves(k) * sin) * (HEAD_DIM ** -0.5)

    def per_problem(fn):
        return jnp.stack([fn(slice(ci * c, (ci + 1) * c), pi) for ci in range(n_ch) for pi in range(N_PAIRS)])

    def stack(x):
        return jnp.concatenate([jnp.where(m_a, x, 0.0), jnp.where(m_b, x, 0.0)], axis=0)

    k_b = per_problem(lambda rows, pi: k[rows, pairs[pi]]).astype(BF16)
    v_b = per_problem(lambda rows, pi: v[rows, pairs[pi]]).astype(BF16)
    q_st = per_problem(lambda rows, pi: stack(q[rows, pairs[pi]])).astype(BF16)
    v_st = per_problem(lambda rows, pi: stack(v[rows, pairs[pi]])).astype(BF16)
    q_w = per_problem(lambda rows, pi: q[rows, pairs[pi]] * qw_ref[:, pairs[pi]]).astype(BF16)
    kw_t = jnp.swapaxes(per_problem(lambda rows, pi: k[rows, pairs[pi]] * kw_ref[:, pairs[pi]]), 1, 2).astype(BF16)
    decay = per_problem(lambda rows, pi: jnp.concatenate([dmat_ref[2 * pi], dmat_ref[2 * pi + 1]], axis=0))

    s = jnp.einsum('nik,njk->nij', q_st, k_b, preferred_element_type=F32) * decay
    intra = _bmm(jnp.concatenate([s[:, :c], s[:, c:]], axis=2), v_st)
    kv = _bmm(kw_t, v_b)
    states = []
    for ci in range(n_ch):
        for pi in range(N_PAIRS):
            st = st_ref[pi]
            states.append(st)
            st_ref[pi] = st * gch_ref[:, pairs[pi]] + jnp.where(same_head, kv[ci * N_PAIRS + pi], 0.0)
    y_all = intra + _bmm(q_w, jnp.stack(states))
    y = jnp.concatenate([jnp.concatenate([y_all[ci * N_PAIRS + pi] for pi in range(N_PAIRS)], axis=1)
                         for ci in range(n_ch)], axis=0)
    yn = jnp.concatenate([_head_norm_pair(y[:, ln], same_head, LN_EPS) for ln in pairs], axis=1)
    o_ref[...] = yn * _silu(p_ref[:, 3 * RET_W:4 * RET_W])


def _retention(p_ret, consts, b, s):
    tb = min(512, s)
    cos_t, sin_t, dmat, qw, kw, gch = consts
    full = lambda a: pl.BlockSpec(a.shape, lambda i, j: (0,) * a.ndim)
    return pl.pallas_call(
        _retention_kernel, grid=(b, s // tb),
        in_specs=[pl.BlockSpec((None, tb, RET_SLAB), lambda i, j: (i, j, 0)),
                  pl.BlockSpec((tb, LANES), lambda i, j: (j, 0)), pl.BlockSpec((tb, LANES), lambda i, j: (j, 0)),
                  full(dmat), full(qw), full(kw), full(gch)],
        out_specs=pl.BlockSpec((None, tb, RET_W), lambda i, j: (i, j, 0)),
        out_shape=jax.ShapeDtypeStruct((b, s, RET_W), F32),
        scratch_shapes=[pltpu.VMEM((N_PAIRS, LANES, LANES), F32)],
        compiler_params=_cparams(2), name="retention")(p_ret.reshape(b, s, RET_SLAB), cos_t, sin_t, dmat, qw, kw, gch)


def _retention_consts(s):
    half = HEAD_DIM // 2
    inv = 1.0 / (ROPE_BASE ** (jnp.arange(0, HEAD_DIM, 2, dtype=F32) / HEAD_DIM))
    ang = jnp.arange(s, dtype=F32)[:, None] * inv[None, :]
    cos, sin = jnp.cos(ang), jnp.sin(ang)
    cos_t = jnp.tile(cos, (1, LANES // half))
    sin_t = jnp.tile(jnp.concatenate([-sin, sin], axis=-1), (1, LANES // HEAD_DIM))
    c = RET_CHUNK
    log_g = jnp.log1p(-jnp.exp2(-5.0 - jnp.arange(RET_H, dtype=F32)))
    pos = jnp.arange(c, dtype=F32)
    diff = pos[:, None] - pos[None, :]
    causal = diff >= 0
    dmat = jnp.where(causal[None], jnp.exp(jnp.where(causal, diff, 0.0)[None] * log_g[:, None, None]), 0.0)
    lane_g = jnp.repeat(log_g, HEAD_DIM)[None, :]
    qw = jnp.exp((pos + 1.0)[:, None] * lane_g)
    kw = jnp.exp((c - 1.0 - pos)[:, None] * lane_g)
    gch = jnp.exp(c * lane_g)
    return cos_t, sin_t, dmat, qw, kw, gch


def _rwkv_kernel(p_ref, mu_ref, wwa_ref, w0_ref, a0_ref, g2_ref, kk_ref, ka_ref, rk_ref, gng_ref, gnb_ref,
                 o_ref, st_ref, carry_ref, rm_s, g_s, y0_s):
    tb = p_ref.shape[0]
    c = RWKV_CHUNK
    w = RWKV_W
    n_ch = tb // c
    pairs = [slice(pi * LANES, (pi + 1) * LANES) for pi in range(N_PAIRS)]

    @pl.when(pl.program_id(1) == 0)
    def _():
        st_ref[...] = jnp.zeros(st_ref.shape, F32)
        carry_ref[...] = jnp.zeros(carry_ref.shape, F32)

    m_a, row, col, same_head = _pair_masks()
    m_b = jnp.logical_not(m_a)
    head_ones = jnp.where(same_head, 1.0, 0.0).astype(BF16)

    def head_sums(x):
        return jnp.concatenate([_split_dot(x[:, ln], head_ones, 2, True) for ln in pairs], axis=1)

    p = p_ref[...]
    rowi = lax.broadcasted_iota(I32, (tb, 1), 0)
    prev = jnp.where(rowi == 0, carry_ref[...], pltpu.roll(p, 1, 0))
    carry_ref[...] = p[tb - 1:tb, :]
    z = p + (prev - p) * mu_ref[...]
    r = z[:, 0:w]
    kr = z[:, w:2 * w]
    vr = z[:, 2 * w:3 * w]
    wa = z[:, 3 * w:3 * w + LANES]
    gl = z[:, 3 * w + LANES:]
    lane = lax.broadcasted_iota(I32, (1, LANES), 1)
    pre = _dot_bf(jnp.where(lane < RWKV_W_LORA, jnp.tanh(wa), wa), wwa_ref[...])
    w_log = -_softplus(-(w0_ref[...] + pre[:, :w])) - 0.5
    a = _sigmoid(a0_ref[...] + pre[:, w:])
    g_rw = _dot_bf(_sigmoid(gl), g2_ref[...])
    kk = kr * kk_ref[...]
    kk = kk / jnp.maximum(jnp.sqrt(head_sums(kk * kk)), 1e-12)
    kmod = kr * (1.0 + (a - 1.0) * ka_ref[...])
    bonus = head_sums(r * kmod * rk_ref[...]) * vr
    ld = -jnp.exp(w_log)

    t_r = lax.broadcasted_iota(I32, (tb, tb), 0)
    t_c = lax.broadcasted_iota(I32, (tb, tb), 1)
    same_chunk = (t_r >> RWKV_CHUNK_SHIFT) == (t_c >> RWKV_CHUNK_SHIFT)
    lc = _split_dot(jnp.where(same_chunk & (t_r >= t_c), 1.0, 0.0).astype(BF16), ld, 3, False)
    lc_end = _split_dot(jnp.where(same_chunk, 1.0, 0.0).astype(BF16), ld, 3, False)
    p_inv = jnp.exp(-lc)
    p_rem = jnp.exp(lc_end - lc)
    p_end = jnp.exp(lc_end)
    b0 = kk * a

    def batch(x):
        return jnp.stack([jnp.concatenate([jnp.where(m_a, x[ci * c:(ci + 1) * c, ln], 0.0),
                                           jnp.where(m_b, x[ci * c:(ci + 1) * c, ln], 0.0)], axis=0)
                          for ci in range(n_ch) for ln in pairs])

    lr = batch(r * jnp.exp(lc))
    la_b = batch(-kk * jnp.exp(lc - ld)).astype(BF16)
    lr_b = lr.astype(BF16)
    rb_b = batch(b0 * p_inv).astype(BF16)
    rk_b = batch(kmod * p_inv).astype(BF16)
    vs_b = batch(vr).astype(BF16)
    be_t = jnp.swapaxes(batch(b0 * p_rem), 1, 2)
    ke_t = jnp.swapaxes(batch(kmod * p_rem), 1, 2)
    pe = jnp.stack([p_end[ci * c:ci * c + 1, ln] for ci in range(n_ch) for ln in pairs])

    tri_incl = (row >= col)[None]
    tri_strict = (row > col)[None]
    eye = (row == col)[None]
    xx = jnp.einsum('nik,njk->nij', jnp.concatenate([la_b, lr_b], axis=1), jnp.concatenate([rb_b, rk_b], axis=1),
                    preferred_element_type=F32)
    x_ab = jnp.where(tri_strict, xx[:, :LANES, :LANES], 0.0)
    x_ak = jnp.where(tri_strict, xx[:, :LANES, LANES:], 0.0)
    x_rb = jnp.where(tri_incl, xx[:, LANES:, :LANES], 0.0)
    x_rk = jnp.where(tri_incl, xx[:, LANES:, LANES:], 0.0)
    inv = None
    for sh in range(RWKV_CHUNK_SHIFT):
        lv = (((row >> (sh + 1)) == (col >> (sh + 1))) & (((row >> sh) & 1) == 1) & (((col >> sh) & 1) == 0))[None]
        x_lv = jnp.where(lv, x_ab, 0.0)
        inv = jnp.where(eye, 1.0, 0.0) + x_lv if inv is None else inv + _bmm(_bmm(inv, x_lv), inv)
    wy = _bmm(jnp.concatenate([x_ak, x_rk], axis=1), vs_b)
    aw_b = _bmm(inv, jnp.concatenate([la_b, wy[:, :LANES].astype(BF16)], axis=2)).astype(BF16)
    ba = _bmm(be_t, aw_b)
    m_mat = jnp.where(eye, pe, 0.0) + ba[:, :, :LANES]
    ra = _bmm(x_rb, aw_b)
    rm_s[...] = jnp.concatenate([lr + ra[:, :, :LANES], m_mat], axis=1).astype(BF16)
    g_s[...] = ba[:, :, LANES:] + _bmm(ke_t, vs_b)
    y0_s[...] = ra[:, :, LANES:] + wy[:, LANES:]

    st = [st_ref[pi] for pi in range(N_PAIRS)]
    y_rows = []
    for ci in range(n_ch):
        ys = []
        for pi in range(N_PAIRS):
            n = ci * N_PAIRS + pi
            sy = jnp.dot(rm_s[n], st[pi].astype(BF16), preferred_element_type=F32)
            yst = sy[:LANES] + y0_s[n]
            st[pi] = sy[LANES:] + g_s[n]
            ys.append(yst[:c] + yst[c:])
        y_rows.append(jnp.concatenate(ys, axis=1))
    for pi in range(N_PAIRS):
        st_ref[pi] = st[pi]
    y = jnp.concatenate(y_rows, axis=0)

    yn = jnp.concatenate([_head_norm_pair(y[:, ln], same_head, RWKV_GN_EPS) for ln in pairs], axis=1)
    o_ref[...] = (yn * gng_ref[...] + gnb_ref[...] + bonus) * g_rw


def _rwkv(p_rwkv, params, b, s):
    tb = min(256, s)
    n = (tb // RWKV_CHUNK) * N_PAIRS
    full = lambda a: pl.BlockSpec(a.shape, lambda i, j: (0,) * a.ndim)
    return pl.pallas_call(
        _rwkv_kernel, grid=(b, s // tb),
        in_specs=[pl.BlockSpec((None, tb, RWKV_SLAB), lambda i, j: (i, j, 0))] + [full(a) for a in params],
        out_specs=pl.BlockSpec((None, tb, RWKV_W), lambda i, j: (i, j, 0)),
        out_shape=jax.ShapeDtypeStruct((b, s, RWKV_W), F32),
        scratch_shapes=[pltpu.VMEM((N_PAIRS, LANES, LANES), F32), pltpu.VMEM((1, RWKV_SLAB), F32),
                        pltpu.VMEM((n, 2 * LANES, LANES), BF16), pltpu.VMEM((n, LANES, LANES), F32),
                        pltpu.VMEM((n, LANES, LANES), F32)],
        compiler_params=_cparams(2), name="rwkv7")(p_rwkv.reshape(b, s, RWKV_SLAB), *params)


def _rwkv_params(mu, w0, w2, a0, a2, g2, k_k, k_a, r_k, gn_g, gn_b):
    row = lambda v: v.reshape(1, -1)
    z = jnp.zeros((RWKV_W_LORA, RWKV_W), F32)
    wwa = jnp.concatenate([jnp.concatenate([w2, z], axis=1), jnp.concatenate([z, a2], axis=1)], axis=0)
    return (row(mu), wwa.astype(BF16), row(w0), row(a0), g2.astype(BF16), row(k_k), row(k_a), row(r_k), row(gn_g), row(gn_b))


def _lru_kernel(p_ref, cw_ref, cb_ref, wax_ref, bax_ref, lam_ref, o_ref, xs_ref, h_ref):
    tb = p_ref.shape[0]
    w = LRU_W

    @pl.when(pl.program_id(1) == 0)
    def _():
        xs_ref[0:SUBLANES, :] = jnp.zeros((SUBLANES, w), F32)
        h_ref[...] = jnp.zeros(h_ref.shape, F32)

    x = p_ref[:, 0:w]
    gate = p_ref[:, w:]
    xs_ref[SUBLANES:, :] = x
    xl = x * cw_ref[LRU_CONV - 1:LRU_CONV, :] + cb_ref[...]
    for d in range(1, LRU_CONV):
        xl = xl + xs_ref[pl.ds(SUBLANES - d, tb), :] * cw_ref[LRU_CONV - 1 - d:LRU_CONV - d, :]
    xs_ref[0:SUBLANES, :] = x[tb - SUBLANES:, :]
    gates = _dot_bf(xl, wax_ref[...]) + bax_ref[...]
    r_gate = _sigmoid(gates[:, :w])
    i_gate = _sigmoid(gates[:, w:])
    log_a = -LRU_C * r_gate * _softplus(-lam_ref[...])
    a = jnp.exp(log_a)
    bv = jnp.sqrt((1.0 - a) * (1.0 + a)) * (i_gate * xl)
    rowi = lax.broadcasted_iota(I32, (tb, 1), 0)
    d = 1
    while d < tb:
        keep = rowi >= d
        a_sh = jnp.where(keep, pltpu.roll(a, d, 0), 1.0)
        b_sh = jnp.where(keep, pltpu.roll(bv, d, 0), 0.0)
        bv = a * b_sh + bv
        a = a * a_sh
        d *= 2
    h = bv + a * h_ref[...]
    h_ref[...] = h[tb - 1:tb, :]
    o_ref[...] = h * _gelu_tanh(gate)


def _lru(p_lru, params, b, s):
    tb = min(512, s)
    full = lambda a: pl.BlockSpec(a.shape, lambda i, j: (0,) * a.ndim)
    return pl.pallas_call(
        _lru_kernel, grid=(b, s // tb),
        in_specs=[pl.BlockSpec((None, tb, LRU_SLAB), lambda i, j: (i, j, 0))] + [full(a) for a in params],
        out_specs=pl.BlockSpec((None, tb, LRU_W), lambda i, j: (i, j, 0)),
        out_shape=jax.ShapeDtypeStruct((b, s, LRU_W), F32),
        scratch_shapes=[pltpu.VMEM((tb + SUBLANES, LRU_W), F32), pltpu.VMEM((1, LRU_W), F32)],
        compiler_params=_cparams(2), name="rglru")(p_lru.reshape(b, s, LRU_SLAB), *params)


def _lru_params(conv_w, conv_b, wa, ba, wx, bx, lam):
    row = lambda v: v.reshape(1, -1)
    def bd(m):
        out = jnp.zeros((LRU_W, LRU_W), F32)
        for i in range(LRU_H):
            out = out.at[i * HEAD_DIM:(i + 1) * HEAD_DIM, i * HEAD_DIM:(i + 1) * HEAD_DIM].set(m[i])
        return out
    wax = jnp.concatenate([bd(wa), bd(wx)], axis=1)
    return (conv_w, row(conv_b), wax.astype(BF16), jnp.concatenate([row(ba), row(bx)], axis=1), row(lam))


ROW_TILE = D_MODEL // LANES


def _rows_from_tiles(ref, n, lead=()):
    return jnp.concatenate([ref[lead + (pl.ds(k, n, stride=ROW_TILE), slice(None))] for k in range(ROW_TILE)], axis=-1)


def _rows_to_tiles(ref, x):
    for k in range(ROW_TILE):
        ref[pl.ds(k, x.shape[0], stride=ROW_TILE), :] = x[:, k * LANES:(k + 1) * LANES]


def _outproj_kernel(yr_ref, yw_ref, yl_ref, h_ref, w_ref, g_ref, b_ref, o_ref, ot_ref):
    ycat = jnp.concatenate([yr_ref[...], yw_ref[...], yl_ref[...]], axis=-1).astype(BF16)
    mix = jnp.dot(ycat, w_ref[...], preferred_element_type=F32)
    h1 = _layer_norm(ALPHA * h_ref[...] + mix, g_ref[...], b_ref[...])
    o_ref[...] = h1
    _rows_to_tiles(ot_ref, h1)


def _out_projection(y_ret, y_rwkv, y_lru, h, w_bf, g, b):
    t = h.shape[0]
    tm = min(512, t)
    row = lambda w: pl.BlockSpec((tm, w), lambda i: (i, 0))
    full = lambda a: pl.BlockSpec(a.shape, lambda i: (0,) * a.ndim)
    return pl.pallas_call(
        _outproj_kernel, grid=(t // tm,),
        in_specs=[row(RET_W), row(RWKV_W), row(LRU_W), row(D_MODEL), full(w_bf), full(g), full(b)],
        out_specs=[row(D_MODEL), pl.BlockSpec((tm * ROW_TILE, LANES), lambda i: (i, 0))],
        out_shape=[jax.ShapeDtypeStruct((t, D_MODEL), F32), jax.ShapeDtypeStruct((t * ROW_TILE, LANES), F32)],
        compiler_params=_cparams(1), name="out_proj_ln")(y_ret, y_rwkv, y_lru, h, w_bf, g, b)


def _router_kernel(h_ref, whi_ref, wlo_ref, b_ref, ew_ref, ei_ref, cnt_ref):
    @pl.when(pl.program_id(0) == 0)
    def _():
        cnt_ref[...] = jnp.zeros(cnt_ref.shape, F32)

    h = h_ref[...]
    h_hi = h.astype(BF16)
    h_lo = (h - h_hi.astype(F32)).astype(BF16)
    logits = (jnp.dot(h_hi, whi_ref[...], preferred_element_type=F32) + jnp.dot(h_lo, whi_ref[...], preferred_element_type=F32)
              + jnp.dot(h_hi, wlo_ref[...], preferred_element_type=F32)) + b_ref[...]
    lane = lax.broadcasted_iota(I32, logits.shape, 1)
    neg = -jnp.inf
    gmask = lane < N_GROUPS
    gl = jnp.where(gmask, logits, neg)
    gmax = jnp.max(gl, -1, keepdims=True)
    g_sel = jnp.min(jnp.where(gl == gmax, lane, LANES), -1, keepdims=True)
    g_gate = 1.0 / jnp.sum(jnp.where(gmask, jnp.exp(gl - gmax), 0.0), -1, keepdims=True)
    emask = (lane >= N_GROUPS) & (lane < N_GROUPS + N_EXPERTS) & (((lane - N_GROUPS) >> GROUP_SHIFT) == g_sel)
    el = jnp.where(emask, logits, neg)
    emax = jnp.max(el, -1, keepdims=True)
    ee = jnp.where(emask, jnp.exp(el - emax), 0.0)
    pe = ee / jnp.sum(ee, -1, keepdims=True)
    pe = jnp.where(emask, pe, -1.0)
    p1 = jnp.max(pe, -1, keepdims=True)
    i1 = jnp.min(jnp.where(pe == p1, lane, LANES), -1, keepdims=True)
    pe2 = jnp.where(lane == i1, -1.0, pe)
    p2 = jnp.max(pe2, -1, keepdims=True)
    i2 = jnp.min(jnp.where(pe2 == p2, lane, LANES), -1, keepdims=True)
    tot = p1 + p2
    ew_ref[...] = jnp.where(lane == 0, g_gate * (p1 / tot), jnp.where(lane == 1, g_gate * (p2 / tot), 0.0))
    tm = logits.shape[0]
    chosen = jnp.where((lane == i1) | (lane == i2), 1.0, 0.0)
    before = (lax.broadcasted_iota(I32, (tm, tm), 0) > lax.broadcasted_iota(I32, (tm, tm), 1))
    cum = jnp.dot(jnp.where(before, 1.0, 0.0).astype(BF16), chosen.astype(BF16), preferred_element_type=F32) + cnt_ref[0:1, :]
    r1 = jnp.sum(jnp.where(lane == i1, cum, 0.0), -1, keepdims=True).astype(I32)
    r2 = jnp.sum(jnp.where(lane == i2, cum, 0.0), -1, keepdims=True).astype(I32)
    cnt_ref[...] = jnp.broadcast_to(cnt_ref[0:1, :] + jnp.sum(chosen, 0, keepdims=True), cnt_ref.shape)
    ei_ref[...] = jnp.where(lane == 0, i1 - N_GROUPS, jnp.where(lane == 1, i2 - N_GROUPS,
                            jnp.where(lane == 2, r1, jnp.where(lane == 3, r2, 0))))


def _router(h, w_r, b_r):
    t = h.shape[0]
    tm = min(512, t)
    w_hi = w_r.astype(BF16)
    w_lo = (w_r - w_hi.astype(F32)).astype(BF16)
    row = lambda w: pl.BlockSpec((tm, w), lambda i: (i, 0))
    full = lambda a: pl.BlockSpec(a.shape, lambda i: (0,) * a.ndim)
    return pl.pallas_call(
        _router_kernel, grid=(t // tm,), in_specs=[row(D_MODEL), full(w_hi), full(w_lo), full(b_r)],
        out_specs=[row(LANES), row(LANES), pl.BlockSpec((SUBLANES, LANES), lambda i: (0, 0))],
        out_shape=[jax.ShapeDtypeStruct((t, LANES), F32), jax.ShapeDtypeStruct((t, LANES), I32),
                   jax.ShapeDtypeStruct((SUBLANES, LANES), F32)],
        compiler_params=_cparams(1), name="router")(h, w_hi, w_lo, b_r)


DMA_UNROLL = 8


def _tile_copy(src, s_row, dst, d_row, sem):
    return pltpu.make_async_copy(src.at[pl.ds(pl.multiple_of(s_row * ROW_TILE, ROW_TILE), ROW_TILE)],
                                 dst.at[pl.ds(pl.multiple_of(d_row * ROW_TILE, ROW_TILE), ROW_TILE)], sem)


def _expert_kernel(tok_ref, be_ref, nv_ref, nu_ref, ht_ref, wg_ref, wu_ref, wd_ref, o_ref, xbuf, sems, wg_s, wu_s, wd_s):
    i = pl.program_id(0)
    n_used = nu_ref[0]
    used = i < n_used
    new_expert = (i == 0) | (be_ref[i] != be_ref[jnp.maximum(i - 1, 0)])

    def gather(step, slot, fn):
        def body(grp, c):
            for q in range(DMA_UNROLL):
                r = grp * DMA_UNROLL + q
                fn(_tile_copy(ht_ref, tok_ref[step * MOE_BLOCK + r], xbuf.at[slot], r, sems.at[slot]))
            return c
        lax.fori_loop(0, (nv_ref[step] + DMA_UNROLL - 1) // DMA_UNROLL, body, 0)

    @pl.when(i == 0)
    def _():
        xbuf[...] = jnp.zeros(xbuf.shape, F32)
        gather(0, 0, lambda cp: cp.start())

    @pl.when(i + 1 < n_used)
    def _():
        gather(i + 1, (i + 1) & 1, lambda cp: cp.start())

    @pl.when(used & new_expert)
    def _():
        wg_s[...] = wg_ref[...].astype(BF16)
        wu_s[...] = wu_ref[...].astype(BF16)
        wd_s[...] = wd_ref[...].astype(BF16)

    @pl.when(used)
    def _():
        slot = i & 1
        gather(i, slot, lambda cp: cp.wait())
        x = _rows_from_tiles(xbuf, MOE_BLOCK, (slot,)).astype(BF16)
        hid = _silu(jnp.dot(x, wg_s[...], preferred_element_type=F32)) * jnp.dot(x, wu_s[...], preferred_element_type=F32)
        _rows_to_tiles(o_ref, jnp.dot(hid.astype(BF16), wd_s[...], preferred_element_type=F32))

    @pl.when(jnp.logical_not(used))
    def _():
        o_ref[...] = jnp.zeros(o_ref.shape, F32)


def _experts(h_tiles, slot_tok, block_exp, n_valid, n_used, wg, wu, wd, layer):
    n_blocks = block_exp.shape[0]
    w_spec = lambda r, c: pl.BlockSpec((None, None, r, c), lambda i, tk, be, nv, nu: (layer, be[i], 0, 0))
    return pl.pallas_call(
        _expert_kernel,
        grid_spec=pltpu.PrefetchScalarGridSpec(
            num_scalar_prefetch=4, grid=(n_blocks,),
            in_specs=[pl.BlockSpec(memory_space=pl.ANY),
                      w_spec(D_MODEL, D_EXPERT), w_spec(D_MODEL, D_EXPERT), w_spec(D_EXPERT, D_MODEL)],
            out_specs=pl.BlockSpec((MOE_BLOCK * ROW_TILE, LANES), lambda i, tk, be, nv, nu: (i, 0)),
            scratch_shapes=[pltpu.VMEM((2, MOE_BLOCK * ROW_TILE, LANES), F32), pltpu.SemaphoreType.DMA((2,)),
                            pltpu.VMEM((D_MODEL, D_EXPERT), BF16), pltpu.VMEM((D_MODEL, D_EXPERT), BF16),
                            pltpu.VMEM((D_EXPERT, D_MODEL), BF16)]),
        out_shape=jax.ShapeDtypeStruct((n_blocks * MOE_BLOCK * ROW_TILE, LANES), F32),
        compiler_params=_cparams(1), name="moe_experts")(slot_tok, block_exp, n_valid, n_used, h_tiles, wg, wu, wd)


def _combine_kernel(dest_ref, yb_ref, ew_ref, h_ref, g_ref, b_ref, o_ref, buf, sems):
    tc = h_ref.shape[0]
    i = pl.program_id(0)
    n = pl.num_programs(0)

    def copies(step, slot, fn):
        def body(r, c):
            tok = step * tc + r
            for j in range(2):
                fn(_tile_copy(yb_ref, dest_ref[2 * tok + j], buf.at[slot, j], r, sems.at[slot]))
            return c
        lax.fori_loop(0, tc, body, 0, unroll=DMA_UNROLL)

    @pl.when(i == 0)
    def _():
        copies(0, 0, lambda cp: cp.start())

    @pl.when(i + 1 < n)
    def _():
        copies(i + 1, (i + 1) & 1, lambda cp: cp.start())

    slot = i & 1
    copies(i, slot, lambda cp: cp.wait())
    ew = ew_ref[...]
    y = ew[:, 0:1] * _rows_from_tiles(buf, tc, (slot, 0)) + ew[:, 1:2] * _rows_from_tiles(buf, tc, (slot, 1))
    o_ref[...] = _layer_norm(ALPHA * h_ref[...] + y, g_ref[...], b_ref[...])


def _combine(yb_tiles, dest, ew, h, g, b):
    t = h.shape[0]
    tc = min(256, t)
    row = lambda w: pl.BlockSpec((tc, w), lambda i, d: (i, 0))
    full = lambda a: pl.BlockSpec(a.shape, lambda i, d: (0,) * a.ndim)
    return pl.pallas_call(
        _combine_kernel,
        grid_spec=pltpu.PrefetchScalarGridSpec(
            num_scalar_prefetch=1, grid=(t // tc,),
            in_specs=[pl.BlockSpec(memory_space=pl.ANY), row(LANES), row(D_MODEL), full(g), full(b)],
            out_specs=row(D_MODEL),
            scratch_shapes=[pltpu.VMEM((2, 2, tc * ROW_TILE, LANES), F32), pltpu.SemaphoreType.DMA((2,))]),
        out_shape=jax.ShapeDtypeStruct((t, D_MODEL), F32),
        compiler_params=_cparams(1), name="moe_combine_ln")(dest, yb_tiles, ew, h, g, b)


def _moe_plan(ei, cnt, t):
    a = t * 2
    counts = cnt[0, N_GROUPS:N_GROUPS + N_EXPERTS].astype(I32)
    padded = ((counts + MOE_BLOCK - 1) // MOE_BLOCK) * MOE_BLOCK
    pends = jnp.cumsum(padded)
    pstarts = pends - padded
    e_ids = jnp.arange(N_EXPERTS, dtype=I32)[None, None, :]
    start_of = jnp.sum(jnp.where(ei[:, 0:2, None] == e_ids, pstarts[None, None, :], 0), axis=-1)
    e_flat = ei[:, 0:2].reshape(a)
    dest = (start_of + ei[:, 2:4]).reshape(a).astype(I32)
    n_blocks = -(-a // MOE_BLOCK) + N_EXPERTS
    block_row0 = jnp.arange(n_blocks, dtype=I32) * MOE_BLOCK
    block_exp = jnp.minimum(jnp.sum((pends[None, :] <= block_row0[:, None]).astype(I32), axis=1), N_EXPERTS - 1)
    n_used = (pends[-1:] // MOE_BLOCK).astype(I32)
    tok_sorted = jnp.argsort(e_flat, stable=True).astype(I32) >> 1
    in_expert = block_row0 - pstarts[block_exp]
    src = (jnp.cumsum(counts) - counts)[block_exp] + in_expert
    lane_r = jnp.arange(MOE_BLOCK, dtype=I32)[None, :]
    valid = (in_expert[:, None] + lane_r) < counts[block_exp][:, None]
    slot_tok = jnp.where(valid, tok_sorted[jnp.clip(src[:, None] + lane_r, 0, a - 1)], 0).reshape(-1)
    n_valid = jnp.clip(counts[block_exp] - in_expert, 0, MOE_BLOCK)
    return dest, slot_tok.astype(I32), block_exp.astype(I32), n_valid.astype(I32), n_used


def _hier_moe_ln(h, h_tiles, w_r, b_r, wg, wu, wd, layer, g, b):
    t = h.shape[0]
    ew, ei, cnt = _router(h, w_r, b_r)
    dest, slot_tok, block_exp, n_valid, n_used = _moe_plan(ei, cnt, t)
    yb_tiles = _experts(h_tiles, slot_tok, block_exp, n_valid, n_used, wg, wu, wd, layer)
    return _combine(yb_tiles, dest, ew, h, g, b)


def kernel(x, ln_in_g, ln_in_b, w_in, w_out, rwkv_mu, rwkv_w0, rwkv_w2, rwkv_a0, rwkv_a2, rwkv_g2, rwkv_k_k, rwkv_k_a, rwkv_r_k, rwkv_gn_g, rwkv_gn_b, lru_conv_w, lru_conv_b, lru_wa, lru_ba, lru_wx, lru_bx, lru_lambda, ln1_g, ln1_b, moe_wg, moe_bg, moe_we, moe_be, moe_w_gate, moe_w_up, moe_w_down, ln2_g, ln2_b):
    b, s, d = x.shape
    t = b * s
    row = lambda v: v.reshape(1, -1)
    ret_consts = _retention_consts(s)
    h = x.reshape(t, d)
    for l in range(DEPTH):
        w_in_bf = w_in[l].astype(BF16)
        if l == 0:
            h, p_ret, p_rwkv, p_lru = _in_projection(h, w_in_bf, ln=(row(ln_in_g), row(ln_in_b)))
        else:
            p_ret, p_rwkv, p_lru = _in_projection(h, w_in_bf)
        y_ret = _retention(p_ret, ret_consts, b, s)
        y_rwkv = _rwkv(p_rwkv, _rwkv_params(rwkv_mu[l], rwkv_w0[l], rwkv_w2[l], rwkv_a0[l], rwkv_a2[l], rwkv_g2[l],
                                            rwkv_k_k[l], rwkv_k_a[l], rwkv_r_k[l], rwkv_gn_g[l], rwkv_gn_b[l]), b, s)
        y_lru = _lru(p_lru, _lru_params(lru_conv_w[l], lru_conv_b[l], lru_wa[l], lru_ba[l], lru_wx[l], lru_bx[l],
                                        lru_lambda[l]), b, s)
        h, h_tiles = _out_projection(y_ret.reshape(t, RET_W), y_rwkv.reshape(t, RWKV_W), y_lru.reshape(t, LRU_W), h,
                                     w_out[l].astype(BF16), row(ln1_g[l]), row(ln1_b[l]))
        pad = jnp.zeros((d, LANES - N_GROUPS - N_EXPERTS), F32)
        w_r = jnp.concatenate([moe_wg[l], moe_we[l], pad], axis=1)
        b_r = jnp.concatenate([moe_bg[l], moe_be[l], jnp.zeros((LANES - N_GROUPS - N_EXPERTS,), F32)]).reshape(1, LANES)
        h = _hier_moe_ln(h, h_tiles, w_r, b_r, moe_w_gate, moe_w_up, moe_w_down, l, row(ln2_g[l]), row(ln2_b[l]))
    return h.reshape(b, s, d)
```

```python
import jax
import jax.numpy as jnp
from jax import lax
from jax.experimental import pallas as pl
from jax.experimental.pallas import tpu as pltpu

F32 = jnp.float32
BF16 = jnp.bfloat16
I32 = jnp.int32

D_MODEL = 1024
DEPTH = 2
HEAD_DIM = 64
HEAD_SHIFT = 6
RET_H = 6
RET_W = RET_H * HEAD_DIM
RET_CHUNK = 128
ROPE_BASE = 10000.0
RWKV_H = 6
RWKV_W = RWKV_H * HEAD_DIM
RWKV_W_LORA = 64
RWKV_A_LORA = 64
RWKV_G_LORA = 128
RWKV_GN_EPS = 64e-5
LRU_H = 4
LRU_W = LRU_H * HEAD_DIM
LRU_CONV = 4
LRU_C = 8.0
RET_SLAB = 4 * RET_W
RWKV_SLAB = 3 * RWKV_W + RWKV_W_LORA + RWKV_A_LORA + RWKV_G_LORA
LRU_SLAB = 2 * LRU_W
IN_W = RET_SLAB + RWKV_SLAB + LRU_SLAB
N_GROUPS = 4
EXPERTS_PER_GROUP = 8
GROUP_SHIFT = 3
N_EXPERTS = N_GROUPS * EXPERTS_PER_GROUP
D_EXPERT = D_MODEL // 2
MOE_BLOCK = 256
LN_EPS = 1e-5
ALPHA = (2.0 * DEPTH) ** 0.25

LANES = 128
SUBLANES = 8
N_PAIRS = RET_W // LANES
RWKV_CHUNK = 64
RWKV_CHUNK_SHIFT = 6
VMEM_LIMIT = 56 * 1024 * 1024


def _cparams(n_axes, vmem=VMEM_LIMIT):
    return pltpu.CompilerParams(dimension_semantics=("arbitrary",) * n_axes, vmem_limit_bytes=vmem)


def _dot_bf(a, b):
    return jnp.dot(a.astype(BF16), b.astype(BF16), preferred_element_type=F32)


def _split_dot(a, b, passes, split_lhs):
    rem = a if split_lhs else b
    acc = None
    for _ in range(passes):
        piece = rem.astype(BF16)
        part = jnp.dot(piece, b, preferred_element_type=F32) if split_lhs else jnp.dot(a, piece, preferred_element_type=F32)
        acc = part if acc is None else acc + part
        rem = rem - piece.astype(F32)
    return acc


def _bmm(a, b):
    return jnp.einsum('nij,njk->nik', a.astype(BF16), b.astype(BF16), preferred_element_type=F32)


def _sigmoid(x):
    return 1.0 / (1.0 + jnp.exp(-x))


def _silu(x):
    return x * _sigmoid(x)


def _softplus(x):
    return jnp.maximum(x, 0.0) + jnp.log1p(jnp.exp(-jnp.abs(x)))


def _gelu_tanh(x):
    return 0.5 * x * (1.0 + jnp.tanh(0.7978845608028654 * (x + 0.044715 * (x * x * x))))


def _layer_norm(x, g, b):
    mu = jnp.mean(x, -1, keepdims=True)
    xc = x - mu
    var = jnp.mean(xc * xc, -1, keepdims=True)
    return xc * lax.rsqrt(var + LN_EPS) * g + b


def _pair_masks():
    lane = lax.broadcasted_iota(I32, (1, LANES), 1)
    m_a = lane < HEAD_DIM
    row = lax.broadcasted_iota(I32, (LANES, LANES), 0)
    col = lax.broadcasted_iota(I32, (LANES, LANES), 1)
    same_head = (row >> HEAD_SHIFT) == (col >> HEAD_SHIFT)
    return m_a, row, col, same_head


def _head_norm_pair(y, same_head, eps):
    avg = jnp.where(same_head, 1.0 / HEAD_DIM, 0.0).astype(BF16)
    mu = _split_dot(y, avg, 2, True)
    yc = y - mu
    var = _split_dot(yc * yc, avg, 2, True)
    return yc * lax.rsqrt(var + eps)


def _inproj_ln_kernel(x_ref, g_ref, b_ref, w_ref, h_ref, pr_ref, pw_ref, pl_ref):
    h = _layer_norm(x_ref[...], g_ref[...], b_ref[...])
    h_ref[...] = h
    p = jnp.dot(h.astype(BF16), w_ref[...], preferred_element_type=F32)
    pr_ref[...] = p[:, :RET_SLAB]
    pw_ref[...] = p[:, RET_SLAB:RET_SLAB + RWKV_SLAB]
    pl_ref[...] = p[:, RET_SLAB + RWKV_SLAB:]


def _inproj_kernel(h_ref, w_ref, pr_ref, pw_ref, pl_ref):
    p = jnp.dot(h_ref[...].astype(BF16), w_ref[...], preferred_element_type=F32)
    pr_ref[...] = p[:, :RET_SLAB]
    pw_ref[...] = p[:, RET_SLAB:RET_SLAB + RWKV_SLAB]
    pl_ref[...] = p[:, RET_SLAB + RWKV_SLAB:]


def _in_projection(x2d, w_bf, ln=None):
    t = x2d.shape[0]
    tm = min(512, t)
    row = lambda w: pl.BlockSpec((tm, w), lambda i: (i, 0))
    full = lambda a: pl.BlockSpec(a.shape, lambda i: (0,) * a.ndim)
    outs = [jax.ShapeDtypeStruct((t, RET_SLAB), F32), jax.ShapeDtypeStruct((t, RWKV_SLAB), F32),
            jax.ShapeDtypeStruct((t, LRU_SLAB), F32)]
    out_specs = [row(RET_SLAB), row(RWKV_SLAB), row(LRU_SLAB)]
    if ln is None:
        return pl.pallas_call(
            _inproj_kernel, grid=(t // tm,), in_specs=[row(D_MODEL), full(w_bf)], out_specs=out_specs,
            out_shape=outs, compiler_params=_cparams(1), name="in_proj")(x2d, w_bf)
    g, b = ln
    return pl.pallas_call(
        _inproj_ln_kernel, grid=(t // tm,), in_specs=[row(D_MODEL), full(g), full(b), full(w_bf)],
        out_specs=[row(D_MODEL)] + out_specs, out_shape=[jax.ShapeDtypeStruct((t, D_MODEL), F32)] + outs,
        compiler_params=_cparams(1), name="ln_in_proj")(x2d, g, b, w_bf)


def _swap_halves(t):
    n = t.shape[-1]
    lane = lax.broadcasted_iota(I32, (1, n), 1)
    lower = (lane & (HEAD_DIM - 1)) < (HEAD_DIM // 2)
    return jnp.where(lower, pltpu.roll(t, n - HEAD_DIM // 2, 1), pltpu.roll(t, HEAD_DIM // 2, 1))


def _retention_kernel(p_ref, cos_ref, sin_ref, dmat_ref, qw_ref, kw_ref, gch_ref, o_ref, st_ref):
    @pl.when(pl.program_id(1) == 0)
    def _():
        st_ref[...] = jnp.zeros(st_ref.shape, F32)

    tb = p_ref.shape[0]
    c = RET_CHUNK
    n_ch = tb // c
    pairs = [slice(pi * LANES, (pi + 1) * LANES) for pi in range(N_PAIRS)]
    m_a, _, _, same_head = _pair_masks()
    m_b = jnp.logical_not(m_a)
    cos = jnp.concatenate([cos_ref[...]] * N_PAIRS, axis=-1)
    sin = jnp.concatenate([sin_ref[...]] * N_PAIRS, axis=-1)
    q = p_ref[:, 0:RET_W]
    k = p_ref[:, RET_W:2 * RET_W]
    v = p_ref[:, 2 * RET_W:3 * RET_W]
    q = q * cos + _swap_halves(q) * sin
    k = (k * cos + _swap_halves(k) * sin) * (HEAD_DIM ** -0.5)

    def per_problem(fn):
        return jnp.stack([fn(slice(ci * c, (ci + 1) * c), pi) for ci in range(n_ch) for pi in range(N_PAIRS)])

    def stack(x):
        return jnp.concatenate([jnp.where(m_a, x, 0.0), jnp.where(m_b, x, 0.0)], axis=0)

    k_b = per_problem(lambda rows, pi: k[rows, pairs[pi]]).astype(BF16)
    v_b = per_problem(lambda rows, pi: v[rows, pairs[pi]]).astype(BF16)
    q_st = per_problem(lambda rows, pi: stack(q[rows, pairs[pi]])).astype(BF16)
    v_st = per_problem(lambda rows, pi: stack(v[rows, pairs[pi]])).astype(BF16)
    q_w = per_problem(lambda rows, pi: q[rows, pairs[pi]] * qw_ref[:, pairs[pi]]).astype(BF16)
    kw_t = jnp.swapaxes(per_problem(lambda rows, pi: k[rows, pairs[pi]] * kw_ref[:, pairs[pi]]), 1, 2).astype(BF16)
    decay = per_problem(lambda rows, pi: jnp.concatenate([dmat_ref[2 * pi], dmat_ref[2 * pi + 1]], axis=0))

    s = jnp.einsum('nik,njk->nij', q_st, k_b, preferred_element_type=F32) * decay
    intra = _bmm(jnp.concatenate([s[:, :c], s[:, c:]], axis=2), v_st)
    kv = _bmm(kw_t, v_b)
    states = []
    for ci in range(n_ch):
        for pi in range(N_PAIRS):
            st = st_ref[pi]
            states.append(st)
            st_ref[pi] = st * gch_ref[:, pairs[pi]] + jnp.where(same_head, kv[ci * N_PAIRS + pi], 0.0)
    y_all = intra + _bmm(q_w, jnp.stack(states))
    y = jnp.concatenate([jnp.concatenate([y_all[ci * N_PAIRS + pi] for pi in range(N_PAIRS)], axis=1)
                         for ci in range(n_ch)], axis=0)
    yn = jnp.concatenate([_head_norm_pair(y[:, ln], same_head, LN_EPS) for ln in pairs], axis=1)
    o_ref[...] = yn * _silu(p_ref[:, 3 * RET_W:4 * RET_W])


def _retention(p_ret, consts, b, s):
    tb = min(512, s)
    cos_t, sin_t, dmat, qw, kw, gch = consts
    full = lambda a: pl.BlockSpec(a.shape, lambda i, j: (0,) * a.ndim)
    return pl.pallas_call(
        _retention_kernel, grid=(b, s // tb),
        in_specs=[pl.BlockSpec((None, tb, RET_SLAB), lambda i, j: (i, j, 0)),
                  pl.BlockSpec((tb, LANES), lambda i, j: (j, 0)), pl.BlockSpec((tb, LANES), lambda i, j: (j, 0)),
                  full(dmat), full(qw), full(kw), full(gch)],
        out_specs=pl.BlockSpec((None, tb, RET_W), lambda i, j: (i, j, 0)),
        out_shape=jax.ShapeDtypeStruct((b, s, RET_W), F32),
        scratch_shapes=[pltpu.VMEM((N_PAIRS, LANES, LANES), F32)],
        compiler_params=_cparams(2), name="retention")(p_ret.reshape(b, s, RET_SLAB), cos_t, sin_t, dmat, qw, kw, gch)


def _retention_consts(s):
    half = HEAD_DIM // 2
    inv = 1.0 / (ROPE_BASE ** (jnp.arange(0, HEAD_DIM, 2, dtype=F32) / HEAD_DIM))
    ang = jnp.arange(s, dtype=F32)[:, None] * inv[None, :]
    cos, sin = jnp.cos(ang), jnp.sin(ang)
    cos_t = jnp.tile(cos, (1, LANES // half))
    sin_t = jnp.tile(jnp.concatenate([-sin, sin], axis=-1), (1, LANES // HEAD_DIM))
    c = RET_CHUNK
    log_g = jnp.log1p(-jnp.exp2(-5.0 - jnp.arange(RET_H, dtype=F32)))
    pos = jnp.arange(c, dtype=F32)
    diff = pos[:, None] - pos[None, :]
    causal = diff >= 0
    dmat = jnp.where(causal[None], jnp.exp(jnp.where(causal, diff, 0.0)[None] * log_g[:, None, None]), 0.0)
    lane_g = jnp.repeat(log_g, HEAD_DIM)[None, :]
    qw = jnp.exp((pos + 1.0)[:, None] * lane_g)
    kw = jnp.exp((c - 1.0 - pos)[:, None] * lane_g)
    gch = jnp.exp(c * lane_g)
    return cos_t, sin_t, dmat, qw, kw, gch


def _rwkv_kernel(p_ref, mu_ref, wwa_ref, w0_ref, a0_ref, g2_ref, kk_ref, ka_ref, rk_ref, gng_ref, gnb_ref,
                 o_ref, st_ref, carry_ref, rm_s, g_s, y0_s):
    tb = p_ref.shape[0]
    c = RWKV_CHUNK
    w = RWKV_W
    n_ch = tb // c
    pairs = [slice(pi * LANES, (pi + 1) * LANES) for pi in range(N_PAIRS)]

    @pl.when(pl.program_id(1) == 0)
    def _():
        st_ref[...] = jnp.zeros(st_ref.shape, F32)
        carry_ref[...] = jnp.zeros(carry_ref.shape, F32)

    m_a, row, col, same_head = _pair_masks()
    m_b = jnp.logical_not(m_a)
    head_ones = jnp.where(same_head, 1.0, 0.0).astype(BF16)

    def head_sums(x):
        return jnp.concatenate([_split_dot(x[:, ln], head_ones, 2, True) for ln in pairs], axis=1)

    p = p_ref[...]
    rowi = lax.broadcasted_iota(I32, (tb, 1), 0)
    prev = jnp.where(rowi == 0, carry_ref[...], pltpu.roll(p, 1, 0))
    carry_ref[...] = p[tb - 1:tb, :]
    z = p + (prev - p) * mu_ref[...]
    r = z[:, 0:w]
    kr = z[:, w:2 * w]
    vr = z[:, 2 * w:3 * w]
    wa = z[:, 3 * w:3 * w + LANES]
    gl = z[:, 3 * w + LANES:]
    lane = lax.broadcasted_iota(I32, (1, LANES), 1)
    pre = _dot_bf(jnp.where(lane < RWKV_W_LORA, jnp.tanh(wa), wa), wwa_ref[...])
    w_log = -_softplus(-(w0_ref[...] + pre[:, :w])) - 0.5
    a = _sigmoid(a0_ref[...] + pre[:, w:])
    g_rw = _dot_bf(_sigmoid(gl), g2_ref[...])
    kk = kr * kk_ref[...]
    kk = kk / jnp.maximum(jnp.sqrt(head_sums(kk * kk)), 1e-12)
    kmod = kr * (1.0 + (a - 1.0) * ka_ref[...])
    bonus = head_sums(r * kmod * rk_ref[...]) * vr
    ld = -jnp.exp(w_log)

    t_r = lax.broadcasted_iota(I32, (tb, tb), 0)
    t_c = lax.broadcasted_iota(I32, (tb, tb), 1)
    same_chunk = (t_r >> RWKV_CHUNK_SHIFT) == (t_c >> RWKV_CHUNK_SHIFT)
    lc = _split_dot(jnp.where(same_chunk & (t_r >= t_c), 1.0, 0.0).astype(BF16), ld, 3, False)
    lc_end = _split_dot(jnp.where(same_chunk, 1.0, 0.0).astype(BF16), ld, 3, False)
    p_inv = jnp.exp(-lc)
    p_rem = jnp.exp(lc_end - lc)
    p_end = jnp.exp(lc_end)
    b0 = kk * a

    def batch(x):
        return jnp.stack([jnp.concatenate([jnp.where(m_a, x[ci * c:(ci + 1) * c, ln], 0.0),
                                           jnp.where(m_b, x[ci * c:(ci + 1) * c, ln], 0.0)], axis=0)
                          for ci in range(n_ch) for ln in pairs])

    lr = batch(r * jnp.exp(lc))
    la_b = batch(-kk * jnp.exp(lc - ld)).astype(BF16)
    lr_b = lr.astype(BF16)
    rb_b = batch(b0 * p_inv).astype(BF16)
    rk_b = batch(kmod * p_inv).astype(BF16)
    vs_b = batch(vr).astype(BF16)
    be_t = jnp.swapaxes(batch(b0 * p_rem), 1, 2)
    ke_t = jnp.swapaxes(batch(kmod * p_rem), 1, 2)
    pe = jnp.stack([p_end[ci * c:ci * c + 1, ln] for ci in range(n_ch) for ln in pairs])

    tri_incl = (row >= col)[None]
    tri_strict = (row > col)[None]
    eye = (row == col)[None]
    xx = jnp.einsum('nik,njk->nij', jnp.concatenate([la_b, lr_b], axis=1), jnp.concatenate([rb_b, rk_b], axis=1),
                    preferred_element_type=F32)
    x_ab = jnp.where(tri_strict, xx[:, :LANES, :LANES], 0.0)
    x_ak = jnp.where(tri_strict, xx[:, :LANES, LANES:], 0.0)
    x_rb = jnp.where(tri_incl, xx[:, LANES:, :LANES], 0.0)
    x_rk = jnp.where(tri_incl, xx[:, LANES:, LANES:], 0.0)
    inv = None
    for sh in range(RWKV_CHUNK_SHIFT):
        lv = (((row >> (sh + 1)) == (col >> (sh + 1))) & (((row >> sh) & 1) == 1) & (((col >> sh) & 1) == 0))[None]
        x_lv = jnp.where(lv, x_ab, 0.0)
        inv = jnp.where(eye, 1.0, 0.0) + x_lv if inv is None else inv + _bmm(_bmm(inv, x_lv), inv)
    wy = _bmm(jnp.concatenate([x_ak, x_rk], axis=1), vs_b)
    aw_b = _bmm(inv, jnp.concatenate([la_b, wy[:, :LANES].astype(BF16)], axis=2)).astype(BF16)
    ba = _bmm(be_t, aw_b)
    m_mat = jnp.where(eye, pe, 0.0) + ba[:, :, :LANES]
    ra = _bmm(x_rb, aw_b)
    rm_s[...] = jnp.concatenate([lr + ra[:, :, :LANES], m_mat], axis=1).astype(BF16)
    g_s[...] = ba[:, :, LANES:] + _bmm(ke_t, vs_b)
    y0_s[...] = ra[:, :, LANES:] + wy[:, LANES:]

    st = [st_ref[pi] for pi in range(N_PAIRS)]
    y_rows = []
    for ci in range(n_ch):
        ys = []
        for pi in range(N_PAIRS):
            n = ci * N_PAIRS + pi
            sy = jnp.dot(rm_s[n], st[pi].astype(BF16), preferred_element_type=F32)
            yst = sy[:LANES] + y0_s[n]
            st[pi] = sy[LANES:] + g_s[n]
            ys.append(yst[:c] + yst[c:])
        y_rows.append(jnp.concatenate(ys, axis=1))
    for pi in range(N_PAIRS):
        st_ref[pi] = st[pi]
    y = jnp.concatenate(y_rows, axis=0)

    yn = jnp.concatenate([_head_norm_pair(y[:, ln], same_head, RWKV_GN_EPS) for ln in pairs], axis=1)
    o_ref[...] = (yn * gng_ref[...] + gnb_ref[...] + bonus) * g_rw


def _rwkv(p_rwkv, params, b, s):
    tb = min(256, s)
    n = (tb // RWKV_CHUNK) * N_PAIRS
    full = lambda a: pl.BlockSpec(a.shape, lambda i, j: (0,) * a.ndim)
    return pl.pallas_call(
        _rwkv_kernel, grid=(b, s // tb),
        in_specs=[pl.BlockSpec((None, tb, RWKV_SLAB), lambda i, j: (i, j, 0))] + [full(a) for a in params],
        out_specs=pl.BlockSpec((None, tb, RWKV_W), lambda i, j: (i, j, 0)),
        out_shape=jax.ShapeDtypeStruct((b, s, RWKV_W), F32),
        scratch_shapes=[pltpu.VMEM((N_PAIRS, LANES, LANES), F32), pltpu.VMEM((1, RWKV_SLAB), F32),
                        pltpu.VMEM((n, 2 * LANES, LANES), BF16), pltpu.VMEM((n, LANES, LANES), F32),
                        pltpu.VMEM((n, LANES, LANES), F32)],
        compiler_params=_cparams(2), name="rwkv7")(p_rwkv.reshape(b, s, RWKV_SLAB), *params)


def _rwkv_params(mu, w0, w2, a0, a2, g2, k_k, k_a, r_k, gn_g, gn_b):
    row = lambda v: v.reshape(1, -1)
    z = jnp.zeros((RWKV_W_LORA, RWKV_W), F32)
    wwa = jnp.concatenate([jnp.concatenate([w2, z], axis=1), jnp.concatenate([z, a2], axis=1)], axis=0)
    return (row(mu), wwa.astype(BF16), row(w0), row(a0), g2.astype(BF16), row(k_k), row(k_a), row(r_k), row(gn_g), row(gn_b))


def _lru_kernel(p_ref, cw_ref, cb_ref, wax_ref, bax_ref, lam_ref, o_ref, xs_ref, h_ref):
    tb = p_ref.shape[0]
    w = LRU_W

    @pl.when(pl.program_id(1) == 0)
    def _():
        xs_ref[0:SUBLANES, :] = jnp.zeros((SUBLANES, w), F32)
        h_ref[...] = jnp.zeros(h_ref.shape, F32)

    x = p_ref[:, 0:w]
    gate = p_ref[:, w:]
    xs_ref[SUBLANES:, :] = x
    xl = x * cw_ref[LRU_CONV - 1:LRU_CONV, :] + cb_ref[...]
    for d in range(1, LRU_CONV):
        xl = xl + xs_ref[pl.ds(SUBLANES - d, tb), :] * cw_ref[LRU_CONV - 1 - d:LRU_CONV - d, :]
    xs_ref[0:SUBLANES, :] = x[tb - SUBLANES:, :]
    gates = _dot_bf(xl, wax_ref[...]) + bax_ref[...]
    r_gate = _sigmoid(gates[:, :w])
    i_gate = _sigmoid(gates[:, w:])
    log_a = -LRU_C * r_gate * _softplus(-lam_ref[...])
    a = jnp.exp(log_a)
    bv = jnp.sqrt((1.0 - a) * (1.0 + a)) * (i_gate * xl)
    rowi = lax.broadcasted_iota(I32, (tb, 1), 0)
    d = 1
    while d < tb:
        keep = rowi >= d
        a_sh = jnp.where(keep, pltpu.roll(a, d, 0), 1.0)
        b_sh = jnp.where(keep, pltpu.roll(bv, d, 0), 0.0)
        bv = a * b_sh + bv
        a = a * a_sh
        d *= 2
    h = bv + a * h_ref[...]
    h_ref[...] = h[tb - 1:tb, :]
    o_ref[...] = h * _gelu_tanh(gate)


def _lru(p_lru, params, b, s):
    tb = min(512, s)
    full = lambda a: pl.BlockSpec(a.shape, lambda i, j: (0,) * a.ndim)
    return pl.pallas_call(
        _lru_kernel, grid=(b, s // tb),
        in_specs=[pl.BlockSpec((None, tb, LRU_SLAB), lambda i, j: (i, j, 0))] + [full(a) for a in params],
        out_specs=pl.BlockSpec((None, tb, LRU_W), lambda i, j: (i, j, 0)),
        out_shape=jax.ShapeDtypeStruct((b, s, LRU_W), F32),
        scratch_shapes=[pltpu.VMEM((tb + SUBLANES, LRU_W), F32), pltpu.VMEM((1, LRU_W), F32)],
        compiler_params=_cparams(2), name="rglru")(p_lru.reshape(b, s, LRU_SLAB), *params)


def _lru_params(conv_w, conv_b, wa, ba, wx, bx, lam):
    row = lambda v: v.reshape(1, -1)
    def bd(m):
        out = jnp.zeros((LRU_W, LRU_W), F32)
        for i in range(LRU_H):
            out = out.at[i * HEAD_DIM:(i + 1) * HEAD_DIM, i * HEAD_DIM:(i + 1) * HEAD_DIM].set(m[i])
        return out
    wax = jnp.concatenate([bd(wa), bd(wx)], axis=1)
    return (conv_w, row(conv_b), wax.astype(BF16), jnp.concatenate([row(ba), row(bx)], axis=1), row(lam))


ROW_TILE = D_MODEL // LANES


def _rows_from_tiles(ref, n, lead=()):
    return jnp.concatenate([ref[lead + (pl.ds(k, n, stride=ROW_TILE), slice(None))] for k in range(ROW_TILE)], axis=-1)


def _rows_to_tiles(ref, x):
    for k in range(ROW_TILE):
        ref[pl.ds(k, x.shape[0], stride=ROW_TILE), :] = x[:, k * LANES:(k + 1) * LANES]


def _outproj_kernel(yr_ref, yw_ref, yl_ref, h_ref, w_ref, g_ref, b_ref, o_ref, ot_ref):
    ycat = jnp.concatenate([yr_ref[...], yw_ref[...], yl_ref[...]], axis=-1).astype(BF16)
    mix = jnp.dot(ycat, w_ref[...], preferred_element_type=F32)
    h1 = _layer_norm(ALPHA * h_ref[...] + mix, g_ref[...], b_ref[...])
    o_ref[...] = h1
    _rows_to_tiles(ot_ref, h1)


def _out_projection(y_ret, y_rwkv, y_lru, h, w_bf, g, b):
    t = h.shape[0]
    tm = min(512, t)
    row = lambda w: pl.BlockSpec((tm, w), lambda i: (i, 0))
    full = lambda a: pl.BlockSpec(a.shape, lambda i: (0,) * a.ndim)
    return pl.pallas_call(
        _outproj_kernel, grid=(t // tm,),
        in_specs=[row(RET_W), row(RWKV_W), row(LRU_W), row(D_MODEL), full(w_bf), full(g), full(b)],
        out_specs=[row(D_MODEL), pl.BlockSpec((tm * ROW_TILE, LANES), lambda i: (i, 0))],
        out_shape=[jax.ShapeDtypeStruct((t, D_MODEL), F32), jax.ShapeDtypeStruct((t * ROW_TILE, LANES), F32)],
        compiler_params=_cparams(1), name="out_proj_ln")(y_ret, y_rwkv, y_lru, h, w_bf, g, b)


def _router_kernel(h_ref, whi_ref, wlo_ref, b_ref, ew_ref, ei_ref, cnt_ref):
    @pl.when(pl.program_id(0) == 0)
    def _():
        cnt_ref[...] = jnp.zeros(cnt_ref.shape, F32)

    h = h_ref[...]
    h_hi = h.astype(BF16)
    h_lo = (h - h_hi.astype(F32)).astype(BF16)
    logits = (jnp.dot(h_hi, whi_ref[...], preferred_element_type=F32) + jnp.dot(h_lo, whi_ref[...], preferred_element_type=F32)
              + jnp.dot(h_hi, wlo_ref[...], preferred_element_type=F32)) + b_ref[...]
    lane = lax.broadcasted_iota(I32, logits.shape, 1)
    neg = -jnp.inf
    gmask = lane < N_GROUPS
    gl = jnp.where(gmask, logits, neg)
    gmax = jnp.max(gl, -1, keepdims=True)
    g_sel = jnp.min(jnp.where(gl == gmax, lane, LANES), -1, keepdims=True)
    g_gate = 1.0 / jnp.sum(jnp.where(gmask, jnp.exp(gl - gmax), 0.0), -1, keepdims=True)
    emask = (lane >= N_GROUPS) & (lane < N_GROUPS + N_EXPERTS) & (((lane - N_GROUPS) >> GROUP_SHIFT) == g_sel)
    el = jnp.where(emask, logits, neg)
    emax = jnp.max(el, -1, keepdims=True)
    ee = jnp.where(emask, jnp.exp(el - emax), 0.0)
    pe = ee / jnp.sum(ee, -1, keepdims=True)
    pe = jnp.where(emask, pe, -1.0)
    p1 = jnp.max(pe, -1, keepdims=True)
    i1 = jnp.min(jnp.where(pe == p1, lane, LANES), -1, keepdims=True)
    pe2 = jnp.where(lane == i1, -1.0, pe)
    p2 = jnp.max(pe2, -1, keepdims=True)
    i2 = jnp.min(jnp.where(pe2 == p2, lane, LANES), -1, keepdims=True)
    tot = p1 + p2
    ew_ref[...] = jnp.where(lane == 0, g_gate * (p1 / tot), jnp.where(lane == 1, g_gate * (p2 / tot), 0.0))
    tm = logits.shape[0]
    chosen = jnp.where((lane == i1) | (lane == i2), 1.0, 0.0)
    before = (lax.broadcasted_iota(I32, (tm, tm), 0) > lax.broadcasted_iota(I32, (tm, tm), 1))
    cum = jnp.dot(jnp.where(before, 1.0, 0.0).astype(BF16), chosen.astype(BF16), preferred_element_type=F32) + cnt_ref[0:1, :]
    r1 = jnp.sum(jnp.where(lane == i1, cum, 0.0), -1, keepdims=True).astype(I32)
    r2 = jnp.sum(jnp.where(lane == i2, cum, 0.0), -1, keepdims=True).astype(I32)
    cnt_ref[...] = jnp.broadcast_to(cnt_ref[0:1, :] + jnp.sum(chosen, 0, keepdims=True), cnt_ref.shape)
    ei_ref[...] = jnp.where(lane == 0, i1 - N_GROUPS, jnp.where(lane == 1, i2 - N_GROUPS,
                            jnp.where(lane == 2, r1, jnp.where(lane == 3, r2, 0))))


def _router(h, w_r, b_r):
    t = h.shape[0]
    tm = min(512, t)
    w_hi = w_r.astype(BF16)
    w_lo = (w_r - w_hi.astype(F32)).astype(BF16)
    row = lambda w: pl.BlockSpec((tm, w), lambda i: (i, 0))
    full = lambda a: pl.BlockSpec(a.shape, lambda i: (0,) * a.ndim)
    return pl.pallas_call(
        _router_kernel, grid=(t // tm,), in_specs=[row(D_MODEL), full(w_hi), full(w_lo), full(b_r)],
        out_specs=[row(LANES), row(LANES), pl.BlockSpec((SUBLANES, LANES), lambda i: (0, 0))],
        out_shape=[jax.ShapeDtypeStruct((t, LANES), F32), jax.ShapeDtypeStruct((t, LANES), I32),
                   jax.ShapeDtypeStruct((SUBLANES, LANES), F32)],
        compiler_params=_cparams(1), name="router")(h, w_hi, w_lo, b_r)


DMA_UNROLL = 8


def _tile_copy(src, s_row, dst, d_row, sem):
    return pltpu.make_async_copy(src.at[pl.ds(pl.multiple_of(s_row * ROW_TILE, ROW_TILE), ROW_TILE)],
                                 dst.at[pl.ds(pl.multiple_of(d_row * ROW_TILE, ROW_TILE), ROW_TILE)], sem)


def _expert_kernel(tok_ref, be_ref, nv_ref, nu_ref, ht_ref, wg_ref, wu_ref, wd_ref, o_ref, xbuf, sems, wg_s, wu_s, wd_s):
    i = pl.program_id(0)
    n_used = nu_ref[0]
    used = i < n_used
    new_expert = (i == 0) | (be_ref[i] != be_ref[jnp.maximum(i - 1, 0)])

    def gather(step, slot, fn):
        def body(grp, c):
            for q in range(DMA_UNROLL):
                r = grp * DMA_UNROLL + q
                fn(_tile_copy(ht_ref, tok_ref[step * MOE_BLOCK + r], xbuf.at[slot], r, sems.at[slot]))
            return c
        lax.fori_loop(0, (nv_ref[step] + DMA_UNROLL - 1) // DMA_UNROLL, body, 0)

    @pl.when(i == 0)
    def _():
        xbuf[...] = jnp.zeros(xbuf.shape, F32)
        gather(0, 0, lambda cp: cp.start())

    @pl.when(i + 1 < n_used)
    def _():
        gather(i + 1, (i + 1) & 1, lambda cp: cp.start())

    @pl.when(used & new_expert)
    def _():
        wg_s[...] = wg_ref[...].astype(BF16)
        wu_s[...] = wu_ref[...].astype(BF16)
        wd_s[...] = wd_ref[...].astype(BF16)

    @pl.when(used)
    def _():
        slot = i & 1
        gather(i, slot, lambda cp: cp.wait())
        x = _rows_from_tiles(xbuf, MOE_BLOCK, (slot,)).astype(BF16)
        hid = _silu(jnp.dot(x, wg_s[...], preferred_element_type=F32)) * jnp.dot(x, wu_s[...], preferred_element_type=F32)
        _rows_to_tiles(o_ref, jnp.dot(hid.astype(BF16), wd_s[...], preferred_element_type=F32))

    @pl.when(jnp.logical_not(used))
    def _():
        o_ref[...] = jnp.zeros(o_ref.shape, F32)


def _experts(h_tiles, slot_tok, block_exp, n_valid, n_used, wg, wu, wd, layer):
    n_blocks = block_exp.shape[0]
    w_spec = lambda r, c: pl.BlockSpec((None, None, r, c), lambda i, tk, be, nv, nu: (layer, be[i], 0, 0))
    return pl.pallas_call(
        _expert_kernel,
        grid_spec=pltpu.PrefetchScalarGridSpec(
            num_scalar_prefetch=4, grid=(n_blocks,),
            in_specs=[pl.BlockSpec(memory_space=pl.ANY),
                      w_spec(D_MODEL, D_EXPERT), w_spec(D_MODEL, D_EXPERT), w_spec(D_EXPERT, D_MODEL)],
            out_specs=pl.BlockSpec((MOE_BLOCK * ROW_TILE, LANES), lambda i, tk, be, nv, nu: (i, 0)),
            scratch_shapes=[pltpu.VMEM((2, MOE_BLOCK * ROW_TILE, LANES), F32), pltpu.SemaphoreType.DMA((2,)),
                            pltpu.VMEM((D_MODEL, D_EXPERT), BF16), pltpu.VMEM((D_MODEL, D_EXPERT), BF16),
                            pltpu.VMEM((D_EXPERT, D_MODEL), BF16)]),
        out_shape=jax.ShapeDtypeStruct((n_blocks * MOE_BLOCK * ROW_TILE, LANES), F32),
        compiler_params=_cparams(1), name="moe_experts")(slot_tok, block_exp, n_valid, n_used, h_tiles, wg, wu, wd)


def _combine_kernel(dest_ref, yb_ref, ew_ref, h_ref, g_ref, b_ref, o_ref, buf, sems):
    tc = h_ref.shape[0]
    i = pl.program_id(0)
    n = pl.num_programs(0)

    def copies(step, slot, fn):
        def body(r, c):
            tok = step * tc + r
            for j in range(2):
                fn(_tile_copy(yb_ref, dest_ref[2 * tok + j], buf.at[slot, j], r, sems.at[slot]))
            return c
        lax.fori_loop(0, tc, body, 0, unroll=DMA_UNROLL)

    @pl.when(i == 0)
    def _():
        copies(0, 0, lambda cp: cp.start())

    @pl.when(i + 1 < n)
    def _():
        copies(i + 1, (i + 1) & 1, lambda cp: cp.start())

    slot = i & 1
    copies(i, slot, lambda cp: cp.wait())
    ew = ew_ref[...]
    y = ew[:, 0:1] * _rows_from_tiles(buf, tc, (slot, 0)) + ew[:, 1:2] * _rows_from_tiles(buf, tc, (slot, 1))
    o_ref[...] = _layer_norm(ALPHA * h_ref[...] + y, g_ref[...], b_ref[...])


def _combine(yb_tiles, dest, ew, h, g, b):
    t = h.shape[0]
    tc = min(512, t)
    row = lambda w: pl.BlockSpec((tc, w), lambda i, d: (i, 0))
    full = lambda a: pl.BlockSpec(a.shape, lambda i, d: (0,) * a.ndim)
    return pl.pallas_call(
        _combine_kernel,
        grid_spec=pltpu.PrefetchScalarGridSpec(
            num_scalar_prefetch=1, grid=(t // tc,),
            in_specs=[pl.BlockSpec(memory_space=pl.ANY), row(LANES), row(D_MODEL), full(g), full(b)],
            out_specs=row(D_MODEL),
            scratch_shapes=[pltpu.VMEM((2, 2, tc * ROW_TILE, LANES), F32), pltpu.SemaphoreType.DMA((2,))]),
        out_shape=jax.ShapeDtypeStruct((t, D_MODEL), F32),
        compiler_params=_cparams(1), name="moe_combine_ln")(dest, yb_tiles, ew, h, g, b)


def _moe_plan(ei, cnt, t):
    a = t * 2
    counts = cnt[0, N_GROUPS:N_GROUPS + N_EXPERTS].astype(I32)
    padded = ((counts + MOE_BLOCK - 1) // MOE_BLOCK) * MOE_BLOCK
    pends = jnp.cumsum(padded)
    pstarts = pends - padded
    e_flat = ei[:, 0:2].reshape(a)
    dest = (pstarts[e_flat] + ei[:, 2:4].reshape(a)).astype(I32)
    n_blocks = -(-a // MOE_BLOCK) + N_EXPERTS
    block_row0 = jnp.arange(n_blocks, dtype=I32) * MOE_BLOCK
    block_exp = jnp.minimum(jnp.sum((pends[None, :] <= block_row0[:, None]).astype(I32), axis=1), N_EXPERTS - 1)
    n_used = (pends[-1:] // MOE_BLOCK).astype(I32)
    tok_sorted = jnp.argsort(e_flat, stable=True).astype(I32) >> 1
    in_expert = block_row0 - pstarts[block_exp]
    src = (jnp.cumsum(counts) - counts)[block_exp] + in_expert
    lane_r = jnp.arange(MOE_BLOCK, dtype=I32)[None, :]
    valid = (in_expert[:, None] + lane_r) < counts[block_exp][:, None]
    slot_tok = jnp.where(valid, tok_sorted[jnp.clip(src[:, None] + lane_r, 0, a - 1)], 0).reshape(-1)
    n_valid = jnp.clip(counts[block_exp] - in_expert, 0, MOE_BLOCK)
    return dest, slot_tok.astype(I32), block_exp.astype(I32), n_valid.astype(I32), n_used


def _hier_moe_ln(h, h_tiles, w_r, b_r, wg, wu, wd, layer, g, b):
    t = h.shape[0]
    ew, ei, cnt = _router(h, w_r, b_r)
    dest, slot_tok, block_exp, n_valid, n_used = _moe_plan(ei, cnt, t)
    yb_tiles = _experts(h_tiles, slot_tok, block_exp, n_valid, n_used, wg, wu, wd, layer)
    return _combine(yb_tiles, dest, ew, h, g, b)


def kernel(x, ln_in_g, ln_in_b, w_in, w_out, rwkv_mu, rwkv_w0, rwkv_w2, rwkv_a0, rwkv_a2, rwkv_g2, rwkv_k_k, rwkv_k_a, rwkv_r_k, rwkv_gn_g, rwkv_gn_b, lru_conv_w, lru_conv_b, lru_wa, lru_ba, lru_wx, lru_bx, lru_lambda, ln1_g, ln1_b, moe_wg, moe_bg, moe_we, moe_be, moe_w_gate, moe_w_up, moe_w_down, ln2_g, ln2_b):
    b, s, d = x.shape
    t = b * s
    row = lambda v: v.reshape(1, -1)
    ret_consts = _retention_consts(s)
    h = x.reshape(t, d)
    for l in range(DEPTH):
        w_in_bf = w_in[l].astype(BF16)
        if l == 0:
            h, p_ret, p_rwkv, p_lru = _in_projection(h, w_in_bf, ln=(row(ln_in_g), row(ln_in_b)))
        else:
            p_ret, p_rwkv, p_lru = _in_projection(h, w_in_bf)
        y_ret = _retention(p_ret, ret_consts, b, s)
        y_rwkv = _rwkv(p_rwkv, _rwkv_params(rwkv_mu[l], rwkv_w0[l], rwkv_w2[l], rwkv_a0[l], rwkv_a2[l], rwkv_g2[l],
                                            rwkv_k_k[l], rwkv_k_a[l], rwkv_r_k[l], rwkv_gn_g[l], rwkv_gn_b[l]), b, s)
        y_lru = _lru(p_lru, _lru_params(lru_conv_w[l], lru_conv_b[l], lru_wa[l], lru_ba[l], lru_wx[l], lru_bx[l],
                                        lru_lambda[l]), b, s)
        h, h_tiles = _out_projection(y_ret.reshape(t, RET_W), y_rwkv.reshape(t, RWKV_W), y_lru.reshape(t, LRU_W), h,
                                     w_out[l].astype(BF16), row(ln1_g[l]), row(ln1_b[l]))
        pad = jnp.zeros((d, LANES - N_GROUPS - N_EXPERTS), F32)
        w_r = jnp.concatenate([moe_wg[l], moe_we[l], pad], axis=1)
        b_r = jnp.concatenate([moe_bg[l], moe_be[l], jnp.zeros((LANES - N_GROUPS - N_EXPERTS,), F32)]).reshape(1, LANES)
        h = _hier_moe_ln(h, h_tiles, w_r, b_r, moe_w_gate, moe_w_up, moe_w_down, l, row(ln2_g[l]), row(ln2_b[l]))
    return h.reshape(b, s, d)
```
